```python
import math
import jax, jax.numpy as jnp
from jax import lax
import numpy as np

D_MODEL = 1024
BATCH = 32
SEQ = 2048
DEPTH = 4

GRID_W = 64
CTX_LEN = 256
NORM_EPS = 1e-6
N_MOD = 6
HEAD_DIM = 64
N_Q_HEADS = 8
N_KV_HEADS = 2
GROUP = N_Q_HEADS // N_KV_HEADS
ATTN_W = N_Q_HEADS * HEAD_DIM
KV_W = N_KV_HEADS * HEAD_DIM
Q_BLOCK = 128
ROPE_THETA = 10000.0
HYENA_W = D_MODEL // 4
HYENA_SHORT = 3
HYENA_BANDS = 16
HYENA_EMB = 1 + 2 * HYENA_BANDS
HYENA_FFN = 64
HYENA_FAST_DECAY = 0.3
HYENA_SLOW_DECAY = 1.5
HYENA_TARGET = 1e-2
CONF_W = D_MODEL // 4
CONF_K = 31
N_BRANCH = 3
K_OFF = ATTN_W
HY_OFF = K_OFF + 2 * KV_W
CF_OFF = HY_OFF + 3 * HYENA_W
G_OFF = CF_OFF + 2 * CONF_W
IN_W = G_OFF + N_BRANCH * D_MODEL
N_EXPERTS = 16
N_GROUPS = 4
TOPK_GROUPS = 1
TOP_K = 2
EXPERT_FF = D_MODEL // 2

kernel_name = "hybrid_gated_attn_hyena_conformer_moe"

F32 = jnp.float32


def rms_norm(x, g):
    xf = x.astype(F32)
    y = xf * lax.rsqrt(jnp.mean(xf * xf, axis=-1, keepdims=True) + NORM_EPS)
    return (y * g.astype(F32)).astype(x.dtype)


def layer_norm(x, g, b):
    xf = x.astype(F32)
    mu = jnp.mean(xf, axis=-1, keepdims=True)
    var = jnp.mean(jnp.square(xf - mu), axis=-1, keepdims=True)
    y = (xf - mu) * lax.rsqrt(var + NORM_EPS) * g.astype(F32) + b.astype(F32)
    return y.astype(x.dtype)


def modulate(h, shift, scale):
    return h * (1 + scale) + shift


def depthwise_conv(x, w, b):
    k = w.shape[0]
    pad = (k - 1) // 2
    y = lax.conv_general_dilated(x, w[:, None, :].astype(x.dtype), window_strides=(1,),
                                 padding=[(pad, pad)], dimension_numbers=("NWC", "WIO", "NWC"),
                                 feature_group_count=x.shape[-1])
    return y + b.astype(x.dtype)


def heads(z, n):
    return z.reshape(z.shape[:-1] + (n, HEAD_DIM))


def axial_rope_tables(n_tokens):
    rows = n_tokens // GRID_W
    row_ids = jnp.repeat(jnp.arange(rows), GRID_W).astype(F32)
    col_ids = jnp.tile(jnp.arange(GRID_W), rows).astype(F32)
    half = HEAD_DIM // 2
    inv = ROPE_THETA ** (-jnp.arange(0, half, 2, dtype=F32) / half)
    ang_r = row_ids[:, None] * inv
    ang_c = col_ids[:, None] * inv
    ang = jnp.concatenate([ang_r, ang_r, ang_c, ang_c], axis=-1)
    return jnp.cos(ang), jnp.sin(ang)


def apply_rope(x, cos, sin):
    xf = x.astype(F32)
    x1, x2, x3, x4 = jnp.split(xf, 4, axis=-1)
    rot = jnp.concatenate([-x2, x1, -x4, x3], axis=-1)
    return (xf * cos[:, None, :] + rot * sin[:, None, :]).astype(x.dtype)


def split_kv(z_kv, k_gain):
    k = rms_norm(heads(z_kv[..., :KV_W], N_KV_HEADS), k_gain)
    v = heads(z_kv[..., KV_W:], N_KV_HEADS)
    return k, v


def latent_attention(q, k, v, k_ctx, v_ctx):
    b, s = q.shape[:2]
    keys = jnp.concatenate([k, k_ctx], axis=1)
    vals = jnp.concatenate([v, v_ctx], axis=1)
    nblk = s // Q_BLOCK
    qb = q.reshape(b, nblk, Q_BLOCK, N_KV_HEADS, GROUP, HEAD_DIM).transpose(1, 0, 2, 3, 4, 5)
    scale = 1.0 / math.sqrt(HEAD_DIM)

    def block(q_blk):
        sc = jnp.einsum("bqkgd,bskd->bkgqs", q_blk, keys, preferred_element_type=F32) * scale
        p = jax.nn.softmax(sc, axis=-1).astype(vals.dtype)
        return jnp.einsum("bkgqs,bskd->bqkgd", p, vals)

    out = lax.map(block, qb)
    return out.transpose(1, 0, 2, 3, 4, 5).reshape(b, s, ATTN_W)


def context_attention(q, k, v):
    b, n = q.shape[:2]
    qg = q.reshape(b, n, N_KV_HEADS, GROUP, HEAD_DIM)
    sc = jnp.einsum("bqkgd,bskd->bkgqs", qg, k, preferred_element_type=F32) / math.sqrt(HEAD_DIM)
    p = jax.nn.softmax(sc, axis=-1).astype(v.dtype)
    return jnp.einsum("bkgqs,bskd->bqkgd", p, v).reshape(b, n, ATTN_W)


def hyena_filter(n, lp):
    t = jnp.linspace(0.0, 1.0, n, dtype=F32)[:, None]
    w = 2.0 * math.pi * jnp.arange(n, dtype=F32)[:, None] / n
    f = jnp.linspace(1e-4, HYENA_BANDS - 1, HYENA_BANDS, dtype=F32)[None, :]
    feat = jnp.concatenate([t, jnp.cos(f * w), -jnp.sin(f * w)], axis=-1)
    freq = lp["hy_freq"].astype(F32)
    h = jnp.sin(freq * (feat @ lp["hy_w1"].astype(F32) + lp["hy_b1"].astype(F32)))
    h = jnp.sin(freq * (h @ lp["hy_w2"].astype(F32) + lp["hy_b2"].astype(F32)))
    h = h @ lp["hy_w3"].astype(F32)
    max_decay = math.log(HYENA_TARGET) / HYENA_FAST_DECAY
    min_decay = math.log(HYENA_TARGET) / HYENA_SLOW_DECAY
    deltas = jnp.tile(jnp.linspace(min_decay, max_decay, HYENA_W, dtype=F32), 2)
    h = h * jnp.exp(-t * jnp.abs(deltas))
    h = h / jnp.sum(jnp.abs(h), axis=0, keepdims=True)
    return h[:, :HYENA_W], h[:, HYENA_W:]


def bidir_fftconv(u, h_f, h_b, d_bias):
    n = u.shape[1]
    kern = jnp.concatenate([h_f, jnp.zeros((1, HYENA_W), F32), h_b[1:][::-1]], axis=0)
    uf = jnp.fft.rfft(u.astype(F32), n=2 * n, axis=1)
    kf = jnp.fft.rfft(kern, n=2 * n, axis=0)
    y = jnp.fft.irfft(uf * kf[None], n=2 * n, axis=1)[:, :n]
    return (y + u.astype(F32) * d_bias.astype(F32)).astype(u.dtype)


def hyena(z, lp):
    z = depthwise_conv(z, lp["hy_short_w"], lp["hy_short_b"])
    x0, x1, v = jnp.split(z, 3, axis=-1)
    h_f, h_b = hyena_filter(z.shape[1], lp)
    return bidir_fftconv(v * x1, h_f, h_b, lp["hy_bias"]) * x0


def conformer_conv(z, lp):
    a, g = jnp.split(z, 2, axis=-1)
    u = depthwise_conv(a * jax.nn.sigmoid(g), lp["cf_dw_w"], lp["cf_dw_b"])
    return jax.nn.silu(layer_norm(u, lp["cf_ln_g"], lp["cf_ln_b"]))


def branch_merge(z, o_attn, lp):
    o_hy = hyena(z[..., HY_OFF:CF_OFF], lp)
    o_cf = conformer_conv(z[..., CF_OFF:G_OFF], lp)
    g_a, g_h, g_c = jnp.split(jax.nn.sigmoid(z[..., G_OFF:]), N_BRANCH, axis=-1)
    merged = (g_a * (o_attn @ lp["p_attn"]) + g_h * (o_hy @ lp["p_hyena"])
              + g_c * (o_cf @ lp["p_conv"]))
    return merged @ lp["w_out"]


def moe(h, w_router, b_router, w_gate, w_up, w_down):
    shp = h.shape
    t = h.reshape(-1, D_MODEL)
    s = jax.nn.sigmoid((t @ w_router).astype(F32))
    sel = s + b_router.astype(F32)
    grp = sel.reshape(-1, N_GROUPS, N_EXPERTS // N_GROUPS)
    grp_score = jnp.sum(lax.top_k(grp, TOP_K)[0], axis=-1)
    _, gidx = lax.top_k(grp_score, TOPK_GROUPS)
    gmask = jnp.any(gidx[..., None] == jnp.arange(N_GROUPS), axis=1)
    emask = jnp.repeat(gmask, N_EXPERTS // N_GROUPS, axis=1)
    _, eidx = lax.top_k(jnp.where(emask, sel, -jnp.inf), TOP_K)
    wts = jnp.take_along_axis(s, eidx, axis=1)
    wts = wts / jnp.sum(wts, axis=-1, keepdims=True)
    combine = jnp.sum(jax.nn.one_hot(eidx, N_EXPERTS, dtype=F32) * wts[..., None], axis=1).astype(t.dtype)
    out = jnp.zeros_like(t)
    for e in range(N_EXPERTS):
        y = (jax.nn.silu(t @ w_gate[e]) * (t @ w_up[e])) @ w_down[e]
        out = out + combine[:, e:e + 1] * y
    return out.reshape(shp)


def setup_inputs(seed: int = 0) -> dict:
    key = jax.random.key(seed)
    ks = iter(jax.random.split(key, 40))

    def nrm(shape, scale):
        return jax.random.normal(next(ks), shape, F32) * scale

    L, D = DEPTH, D_MODEL
    return {
        "x": nrm((BATCH, SEQ, D), 1.0),
        "c": nrm((BATCH, D), 1.0),
        "ctx": nrm((BATCH, CTX_LEN, D), 1.0),
        "c_ctx": nrm((D,), 1.0),
        "w_mod": nrm((L, D, N_MOD * D), 0.5 * D ** -0.5),
        "b_mod": nrm((L, N_MOD * D), 0.01),
        "g_norm1": 1.0 + nrm((L, D), 0.01),
        "g_norm2": 1.0 + nrm((L, D), 0.01),
        "w_in": nrm((L, D, IN_W), D ** -0.5),
        "q_norm": 1.0 + nrm((L, HEAD_DIM), 0.01),
        "k_norm": 1.0 + nrm((L, HEAD_DIM), 0.01),
        "hy_short_w": nrm((L, HYENA_SHORT, 3 * HYENA_W), HYENA_SHORT ** -0.5),
        "hy_short_b": nrm((L, 3 * HYENA_W), 0.01),
        "hy_w1": nrm((L, HYENA_EMB, HYENA_FFN), HYENA_EMB ** -0.5),
        "hy_b1": nrm((L, HYENA_FFN), 0.01),
        "hy_w2": nrm((L, HYENA_FFN, HYENA_FFN), HYENA_FFN ** -0.5),
        "hy_b2": nrm((L, HYENA_FFN), 0.01),
        "hy_w3": nrm((L, HYENA_FFN, 2 * HYENA_W), HYENA_FFN ** -0.5),
        "hy_freq": 1.0 + nrm((L, HYENA_FFN), 0.01),
        "hy_bias": nrm((L, HYENA_W), 0.1),
        "cf_dw_w": nrm((L, CONF_K, CONF_W), CONF_K ** -0.5),
        "cf_dw_b": nrm((L, CONF_W), 0.01),
        "cf_ln_g": 1.0 + nrm((L, CONF_W), 0.01),
        "cf_ln_b": nrm((L, CONF_W), 0.01),
        "p_attn": nrm((L, ATTN_W, D), ATTN_W ** -0.5),
        "p_hyena": nrm((L, HYENA_W, D), HYENA_W ** -0.5),
        "p_conv": nrm((L, CONF_W, D), CONF_W ** -0.5),
        "w_out": nrm((L, D, D), D ** -0.5),
        "w_router": nrm((D, N_EXPERTS), D ** -0.5),
        "b_router": nrm((N_EXPERTS,), 0.01),
        "w_exp_gate": nrm((L, N_EXPERTS, D, EXPERT_FF), D ** -0.5),
        "w_exp_up": nrm((L, N_EXPERTS, D, EXPERT_FF), D ** -0.5),
        "w_exp_down": nrm((L, N_EXPERTS, EXPERT_FF, D), EXPERT_FF ** -0.5),
    }


def reference(x, c, ctx, c_ctx, w_mod, b_mod, g_norm1, g_norm2, w_in, q_norm, k_norm,
              hy_short_w, hy_short_b, hy_w1, hy_b1, hy_w2, hy_b2, hy_w3, hy_freq, hy_bias,
              cf_dw_w, cf_dw_b, cf_ln_g, cf_ln_b, p_attn, p_hyena, p_conv, w_out,
              w_router, b_router, w_exp_gate, w_exp_up, w_exp_down):
    s = x.shape[1]
    cos, sin = axial_rope_tables(s)
    xc = ctx
    for l in range(DEPTH):
        last = l == DEPTH - 1
        lp = {
            "hy_short_w": hy_short_w[l], "hy_short_b": hy_short_b[l],
            "hy_w1": hy_w1[l], "hy_b1": hy_b1[l], "hy_w2": hy_w2[l], "hy_b2": hy_b2[l],
            "hy_w3": hy_w3[l], "hy_freq": hy_freq[l], "hy_bias": hy_bias[l],
            "cf_dw_w": cf_dw_w[l], "cf_dw_b": cf_dw_b[l], "cf_ln_g": cf_ln_g[l], "cf_ln_b": cf_ln_b[l],
            "p_attn": p_attn[l], "p_hyena": p_hyena[l], "p_conv": p_conv[l], "w_out": w_out[l],
        }
        sh1, sc1, gt1, sh2, sc2, gt2 = jnp.split(
            (jax.nn.silu(c) @ w_mod[l] + b_mod[l])[:, None, :], N_MOD, axis=-1)
        csh1, csc1, cgt1, csh2, csc2, cgt2 = jnp.split(
            jax.nn.silu(c_ctx) @ w_mod[l] + b_mod[l], N_MOD, axis=-1)

        hx = modulate(rms_norm(x, g_norm1[l]), sh1, sc1)
        hc = modulate(rms_norm(xc, g_norm1[l]), csh1, csc1)
        zx = hx @ w_in[l]
        if last:
            zc_kv = hc @ w_in[l][:, K_OFF:HY_OFF]
        else:
            zc = hc @ w_in[l]
            zc_kv = zc[..., K_OFF:HY_OFF]
        kc, vc = split_kv(zc_kv, k_norm[l])
        qx = apply_rope(rms_norm(heads(zx[..., :K_OFF], N_Q_HEADS), q_norm[l]), cos, sin)
        kx, vx = split_kv(zx[..., K_OFF:HY_OFF], k_norm[l])
        kx = apply_rope(kx, cos, sin)
        o_attn_x = latent_attention(qx, kx, vx, kc, vc)
        x = x + gt1 * branch_merge(zx, o_attn_x, lp)

        hx2 = modulate(rms_norm(x, g_norm2[l]), sh2, sc2)
        if last:
            x = x + gt2 * moe(hx2, w_router, b_router, w_exp_gate[l], w_exp_up[l], w_exp_down[l])
        else:
            qc = rms_norm(heads(zc[..., :K_OFF], N_Q_HEADS), q_norm[l])
            o_attn_c = context_attention(qc, kc, vc)
            xc = xc + cgt1 * branch_merge(zc, o_attn_c, lp)
            hc2 = modulate(rms_norm(xc, g_norm2[l]), csh2, csc2)
            y = moe(jnp.concatenate([hx2, hc2], axis=1), w_router, b_router,
                    w_exp_gate[l], w_exp_up[l], w_exp_down[l])
            x = x + gt2 * y[:, :s]
            xc = xc + cgt2 * y[:, s:]
    return x
```

```python
import functools
import math

import jax
import jax.numpy as jnp
from jax import lax
from jax.experimental import pallas as pl
from jax.experimental.pallas import tpu as pltpu

F32 = jnp.float32
BF16 = jnp.bfloat16

NORM_EPS = 1e-6
N_MOD = 6
HEAD_DIM = 64
N_Q_HEADS = 8
N_KV_HEADS = 2
ATTN_W = N_Q_HEADS * HEAD_DIM
KV_W = N_KV_HEADS * HEAD_DIM
GRID_W = 64
ROPE_THETA = 10000.0
HYENA_BANDS = 16
HYENA_FAST_DECAY = 0.3
HYENA_SLOW_DECAY = 1.5
HYENA_TARGET = 1e-2
CONF_K = 31
N_EXPERTS = 16
N_GROUPS = 4
GROUP_SIZE = N_EXPERTS // N_GROUPS

LANES = 128
VMEM_LIMIT = 56 * 1024 * 1024


def _cparams(*sem):
    return pltpu.CompilerParams(dimension_semantics=sem, vmem_limit_bytes=VMEM_LIMIT)


def _dot(a, b):
    return jnp.dot(a, b, preferred_element_type=F32)


def _split(a):
    hi = a.astype(BF16)
    lo = (a - hi.astype(F32)).astype(BF16)
    return hi, lo


def _dot3(a, b):
    ah, al = _split(a)
    bh, bl = _split(b)
    return _dot(ah, bh) + (_dot(al, bh) + _dot(ah, bl))


def _silu(x):
    return x * jax.nn.sigmoid(x)


def _mod_kernel(c_ref, w_ref, b_ref, o_ref):
    o_ref[0] = _dot3(_silu(c_ref[...]), w_ref[0]) + b_ref[0]


def _mod_call(cc, w_mod, b_mod):
    depth, d, n = w_mod.shape
    r = cc.shape[0]
    tn = 1536
    return pl.pallas_call(
        _mod_kernel,
        grid=(depth, n // tn),
        in_specs=[pl.BlockSpec((r, d), lambda l, j: (0, 0)),
                  pl.BlockSpec((1, d, tn), lambda l, j: (l, 0, j)),
                  pl.BlockSpec((1, 1, tn), lambda l, j: (l, 0, j))],
        out_specs=pl.BlockSpec((1, r, tn), lambda l, j: (l, 0, j)),
        out_shape=jax.ShapeDtypeStruct((depth, r, n), F32),
        compiler_params=_cparams("arbitrary", "arbitrary"),
        name="adaln_mod",
    )(cc, w_mod, b_mod.reshape(depth, 1, n))


def _inproj_kernel(x_ref, mod_ref, g_ref, w_ref, cos_ref, sin_ref, qg_ref, kg_ref, hm_ref,
                   q_ref, k_ref, v_ref, zhy_ref, zcf_ref, zg_ref):
    d = x_ref.shape[1]
    x = x_ref[...]
    ms = jnp.mean(x * x, axis=-1, keepdims=True)
    y = x * lax.rsqrt(ms + NORM_EPS) * g_ref[...]
    sh = mod_ref[0, :, 0:d]
    sc = mod_ref[0, :, d:2 * d]
    h = (y * (1.0 + sc) + sh).astype(BF16)

    cos = cos_ref[...]
    sin = sin_ref[...]
    hm = hm_ref[...]
    lane = lax.broadcasted_iota(jnp.int32, (1, LANES), 1)
    first_quarter = (lane % 32) < 16

    def norm_rope(z, gain, scale):
        hi, lo = _split(z * z)
        msq = _dot(hi, hm) + _dot(lo, hm)
        zn = z * lax.rsqrt(msq + NORM_EPS) * gain
        rot = jnp.where(first_quarter, -pltpu.roll(zn, LANES - 16, 1), pltpu.roll(zn, 16, 1))
        return (zn * cos + rot * sin) * scale

    zq = _dot(h, w_ref[:, 0:ATTN_W])
    for cb in range(ATTN_W // LANES):
        sl = slice(cb * LANES, (cb + 1) * LANES)
        q_ref[:, sl] = norm_rope(zq[:, sl], qg_ref[...], 1.0 / math.sqrt(HEAD_DIM)).astype(BF16)
    zkv = _dot(h, w_ref[:, ATTN_W:ATTN_W + 2 * KV_W])
    k_ref[...] = norm_rope(zkv[:, 0:KV_W], kg_ref[...], 1.0).astype(BF16)
    v_ref[...] = zkv[:, KV_W:2 * KV_W].astype(BF16)
    off = ATTN_W + 2 * KV_W
    for ref in (zhy_ref, zcf_ref):
        w = ref.shape[1]
        ref[...] = _dot(h, w_ref[:, off:off + w])
        off += w
    chunk = 1024
    for cb in range(zg_ref.shape[1] // chunk):
        zg_ref[:, cb * chunk:(cb + 1) * chunk] = _dot(
            h, w_ref[:, off + cb * chunk:off + (cb + 1) * chunk])


def _inproj_call(x, mod, g1, w_in, cos, sin, qg, kg, hm, *, tm, hy_w, cf_w):
    t, d = x.shape
    tiles_per_mod = t // mod.shape[0] // tm
    n = w_in.shape[1]
    g_w = n - ATTN_W - 2 * KV_W - hy_w - cf_w
    n_pos = cos.shape[0] // tm
    row = lambda i: (i, 0)
    const = lambda i: (0, 0)
    return pl.pallas_call(
        _inproj_kernel,
        grid=(t // tm,),
        in_specs=[pl.BlockSpec((tm, d), row),
                  pl.BlockSpec((1, 1, mod.shape[2]), lambda i: (i // tiles_per_mod, 0, 0)),
                  pl.BlockSpec((1, d), const),
                  pl.BlockSpec((d, n), const),
                  pl.BlockSpec((tm, LANES), lambda i: (i % n_pos, 0)),
                  pl.BlockSpec((tm, LANES), lambda i: (i % n_pos, 0)),
                  pl.BlockSpec((1, LANES), const),
                  pl.BlockSpec((1, LANES), const),
                  pl.BlockSpec((LANES, LANES), const)],
        out_specs=[pl.BlockSpec((tm, ATTN_W), row),
                   pl.BlockSpec((tm, KV_W), row),
                   pl.BlockSpec((tm, KV_W), row),
                   pl.BlockSpec((tm, hy_w), row),
                   pl.BlockSpec((tm, cf_w), row),
                   pl.BlockSpec((tm, g_w), row)],
        out_shape=[jax.ShapeDtypeStruct((t, ATTN_W), BF16),
                   jax.ShapeDtypeStruct((t, KV_W), BF16),
                   jax.ShapeDtypeStruct((t, KV_W), BF16),
                   jax.ShapeDtypeStruct((t, hy_w), F32),
                   jax.ShapeDtypeStruct((t, cf_w), F32),
                   jax.ShapeDtypeStruct((t, g_w), F32)],
        compiler_params=_cparams("arbitrary"),
        name="in_proj",
    )(x, mod, g1, w_in, cos, sin, qg, kg, hm)


def _attn_kernel(*refs, n_src):
    q_ref = refs[0]
    kv_refs = refs[1:1 + 2 * n_src]
    o_ref = refs[1 + 2 * n_src]
    kk_ref, vv_ref = refs[2 + 2 * n_src:]
    h = pl.program_id(1)
    lane = lax.broadcasted_iota(jnp.int32, (1, LANES), 1)
    lo_half = lane < HEAD_DIM

    @pl.when(pl.program_id(2) == 0)
    def _():
        keep = (lane // HEAD_DIM) == h
        off = 0
        for s in range(n_src):
            n = kv_refs[2 * s].shape[0]
            for src, dst in ((kv_refs[2 * s], kk_ref), (kv_refs[2 * s + 1], vv_ref)):
                a = src[...].astype(F32)
                dst[off:off + n, :] = jnp.where(keep, a, pltpu.roll(a, HEAD_DIM, 1)).astype(BF16)
            off += n

    kk = kk_ref[...]
    vv = vv_ref[...]
    for j in range(q_ref.shape[1] // LANES):
        q2 = q_ref[:, j * LANES:(j + 1) * LANES]
        outs = []
        for half in range(2):
            msk = lo_half if half == 0 else jnp.logical_not(lo_half)
            qm = jnp.where(msk, q2, jnp.zeros_like(q2))
            s = lax.dot_general(qm, kk, (((1,), (1,)), ((), ())), preferred_element_type=F32)
            m = jnp.max(s, axis=-1, keepdims=True)
            p = jnp.exp(s - m)
            l = jnp.sum(p, axis=-1, keepdims=True)
            outs.append(_dot(p.astype(BF16), vv) / l)
        o_ref[:, j * LANES:(j + 1) * LANES] = jnp.where(lo_half, outs[0], outs[1]).astype(BF16)


def _attn_call(q, kvs, *, batch, tq):
    t = q.shape[0]
    nq = t // batch // tq
    gw = ATTN_W // N_KV_HEADS
    in_specs = [pl.BlockSpec((tq, gw), lambda b, h, i: (b * nq + i, h))]
    args = [q]
    nk = 0
    for k, v in kvs:
        n = k.shape[0] // batch
        nk += n
        for a in (k, v):
            in_specs.append(pl.BlockSpec((n, KV_W), lambda b, h, i: (b, 0)))
            args.append(a)
    return pl.pallas_call(
        functools.partial(_attn_kernel, n_src=len(kvs)),
        grid=(batch, N_KV_HEADS, nq),
        in_specs=in_specs,
        out_specs=pl.BlockSpec((tq, gw), lambda b, h, i: (b * nq + i, h)),
        out_shape=jax.ShapeDtypeStruct((t, ATTN_W), BF16),
        scratch_shapes=[pltpu.VMEM((nk, LANES), BF16), pltpu.VMEM((nk, LANES), BF16)],
        compiler_params=_cparams("arbitrary", "arbitrary", "arbitrary"),
        name="attention",
    )(*args)


def _hy_time_kernel(feat_ref, w1_ref, b1_ref, w2_ref, b2_ref, w3_ref, fq_ref, dec_ref,
                    sp_ref, sm_ref, kn_ref):
    n = feat_ref.shape[0]
    fq = fq_ref[0]
    h = jnp.sin(fq * (_dot3(feat_ref[...], w1_ref[0]) + b1_ref[0]))
    h = jnp.sin(fq * (_dot3(h, w2_ref[0]) + b2_ref[0]))
    h = _dot3(h, w3_ref[0]) * dec_ref[...]
    h = h / jnp.sum(jnp.abs(h), axis=0, keepdims=True)
    c = h.shape[1] // 2
    row = lax.broadcasted_iota(jnp.int32, (n, 1), 0)
    hf = h[:, 0:c]
    hb = jnp.where(row == 0, 0.0, h[:, c:2 * c])
    sp = hf + hb
    sp_ref[0] = sp
    sm_ref[0] = hf - hb
    sgn = (1 - 2 * (row & 1)).astype(F32)
    kn_ref[0] = jnp.sum(sp * sgn, axis=0, keepdims=True) * (0.5 / n)


def _hy_spec_kernel(fch_ref, fcl_ref, fsh_ref, fsl_ref, sp_ref, sm_ref, kr_ref, ki_ref, *, n):
    tk = fch_ref.shape[0]
    k = pl.program_id(1) * tk + lax.broadcasted_iota(jnp.int32, (tk, 1), 0)
    scl = jnp.where(k == 0, 0.5 / n, 1.0 / n)
    sph, spl = _split(sp_ref[0])
    smh, sml = _split(sm_ref[0])
    kr = _dot(fch_ref[...], sph) + (_dot(fcl_ref[...], sph) + _dot(fch_ref[...], spl))
    ki = _dot(fsh_ref[...], smh) + (_dot(fsl_ref[...], smh) + _dot(fsh_ref[...], sml))
    kr_ref[0] = kr * scl
    ki_ref[0] = ki * scl


def _dft_tables(n):
    k = lax.broadcasted_iota(jnp.int32, (n, n), 0)
    t = lax.broadcasted_iota(jnp.int32, (n, n), 1)
    ang = ((k * t) % (2 * n)).astype(F32) * (math.pi / n)
    return _split(jnp.cos(ang)) + _split(-jnp.sin(ang))


def _hyena_features(n, c):
    t = jnp.linspace(0.0, 1.0, n, dtype=F32)[:, None]
    w = 2.0 * math.pi * jnp.arange(n, dtype=F32)[:, None] / n
    f = jnp.linspace(1e-4, HYENA_BANDS - 1, HYENA_BANDS, dtype=F32)[None, :]
    feat = jnp.concatenate([t, jnp.cos(f * w), -jnp.sin(f * w)], axis=-1)
    feat = jnp.pad(feat, ((0, 0), (0, LANES - feat.shape[1])))
    max_decay = math.log(HYENA_TARGET) / HYENA_FAST_DECAY
    min_decay = math.log(HYENA_TARGET) / HYENA_SLOW_DECAY
    deltas = jnp.tile(jnp.linspace(min_decay, max_decay, c, dtype=F32), 2)
    return feat, jnp.exp(-t * jnp.abs(deltas))


def _hyena_filter_call(n, tabs, w1p, b1, w2, b2, w3, fq):
    depth = w1p.shape[0]
    ffn = w2.shape[1]
    c = w3.shape[2] // 2
    feat, dec = _hyena_features(n, c)
    lay = lambda l: (l, 0, 0)
    const = lambda l: (0, 0)
    sp, sm, kn = pl.pallas_call(
        _hy_time_kernel,
        grid=(depth,),
        in_specs=[pl.BlockSpec((n, LANES), const),
                  pl.BlockSpec((1, LANES, ffn), lay), pl.BlockSpec((1, 1, ffn), lay),
                  pl.BlockSpec((1, ffn, ffn), lay), pl.BlockSpec((1, 1, ffn), lay),
                  pl.BlockSpec((1, ffn, 2 * c), lay), pl.BlockSpec((1, 1, ffn), lay),
                  pl.BlockSpec((n, 2 * c), const)],
        out_specs=[pl.BlockSpec((1, n, c), lay), pl.BlockSpec((1, n, c), lay),
                   pl.BlockSpec((1, 1, c), lay)],
        out_shape=[jax.ShapeDtypeStruct((depth, n, c), F32), jax.ShapeDtypeStruct((depth, n, c), F32),
                   jax.ShapeDtypeStruct((depth, 1, c), F32)],
        compiler_params=_cparams("arbitrary"),
        name="hyena_filter_time",
    )(feat, w1p, b1, w2, b2, w3, fq, dec)
    tk = min(n, 512)
    mat = pl.BlockSpec((tk, n), lambda l, j: (j, 0))
    vec = pl.BlockSpec((1, n, c), lambda l, j: (l, 0, 0))
    out = pl.BlockSpec((1, tk, c), lambda l, j: (l, j, 0))
    kr, ki = pl.pallas_call(
        functools.partial(_hy_spec_kernel, n=n),
        grid=(depth, n // tk),
        in_specs=[mat, mat, mat, mat, vec, vec],
        out_specs=[out, out],
        out_shape=[jax.ShapeDtypeStruct((depth, n, c), F32)] * 2,
        compiler_params=_cparams("arbitrary", "arbitrary"),
        name="hyena_filter_spectrum",
    )(*tabs, sp, sm)
    return kr, ki, kn


def _hyena_kernel(z_ref, sw_ref, sb_ref, fc_ref, fs_ref, kr_ref, ki_ref, kn_ref, db_ref, o_ref):
    n = z_ref.shape[0]
    c = o_ref.shape[1]
    z = z_ref[...]
    row = lax.broadcasted_iota(jnp.int32, (n, 1), 0)
    z_prev = jnp.where(row == 0, 0.0, pltpu.roll(z, 1, 0))
    z_next = jnp.where(row == n - 1, 0.0, pltpu.roll(z, n - 1, 0))
    zc = z_prev * sw_ref[0:1, :] + z * sw_ref[1:2, :] + z_next * sw_ref[2:3, :] + sb_ref[...]
    x0 = zc[:, 0:c]
    u = zc[:, 2 * c:3 * c] * zc[:, c:2 * c]
    ub = u.astype(BF16)
    fc = fc_ref[...]
    fs = fs_ref[...]
    ur = _dot(fc, ub)
    ui = _dot(fs, ub)
    kr = kr_ref[...]
    ki = ki_ref[...]
    pr = (ur * kr - ui * ki).astype(BF16)
    pi = (ur * ki + ui * kr).astype(BF16)
    sgn = (1 - 2 * (row & 1)).astype(F32)
    nyq = jnp.sum(u * sgn, axis=0, keepdims=True) * kn_ref[...]
    y = _dot(fc, pr) + _dot(fs, pi) + sgn * nyq
    o_ref[...] = ((y + u * db_ref[...]) * x0).astype(BF16)


def _hyena_call(z, sw, sb, fc, fs, kr, ki, kn, db, *, n):
    t, w = z.shape
    c = w // 3
    const = lambda b: (0, 0)
    return pl.pallas_call(
        _hyena_kernel,
        grid=(t // n,),
        in_specs=[pl.BlockSpec((n, w), lambda b: (b, 0)),
                  pl.BlockSpec((3, w), const), pl.BlockSpec((1, w), const),
                  pl.BlockSpec((n, n), const), pl.BlockSpec((n, n), const),
                  pl.BlockSpec((n, c), const), pl.BlockSpec((n, c), const),
                  pl.BlockSpec((1, c), const), pl.BlockSpec((1, c), const)],
        out_specs=pl.BlockSpec((n, c), lambda b: (b, 0)),
        out_shape=jax.ShapeDtypeStruct((t, c), BF16),
        compiler_params=_cparams("arbitrary"),
        name="hyena_conv",
    )(z, sw, sb, fc, fs, kr, ki, kn, db)


CONF_PAD = 16


def _conf_kernel(z_ref, w_ref, b_ref, g_ref, beta_ref, o_ref, pad_ref):
    n = z_ref.shape[0]
    c = o_ref.shape[1]
    u = z_ref[:, 0:c] * jax.nn.sigmoid(z_ref[:, c:2 * c])
    zeros = jnp.zeros((CONF_PAD, c), F32)
    pad_ref[0:CONF_PAD, :] = zeros
    pad_ref[CONF_PAD + n:2 * CONF_PAD + n, :] = zeros
    pad_ref[CONF_PAD:CONF_PAD + n, :] = u
    first = CONF_PAD - (CONF_K - 1) // 2
    acc = jnp.zeros((n, c), F32) + b_ref[...]
    for j in range(CONF_K):
        acc = acc + w_ref[j:j + 1, :] * pad_ref[first + j:first + j + n, :]
    mu = jnp.mean(acc, axis=-1, keepdims=True)
    dlt = acc - mu
    var = jnp.mean(dlt * dlt, axis=-1, keepdims=True)
    y = dlt * lax.rsqrt(var + NORM_EPS) * g_ref[...] + beta_ref[...]
    o_ref[...] = _silu(y).astype(BF16)


def _conf_call(z, w, b, g, beta, *, n):
    t, w2 = z.shape
    c = w2 // 2
    const = lambda i: (0, 0)
    return pl.pallas_call(
        _conf_kernel,
        grid=(t // n,),
        in_specs=[pl.BlockSpec((n, w2), lambda i: (i, 0)),
                  pl.BlockSpec((CONF_K, c), const), pl.BlockSpec((1, c), const),
                  pl.BlockSpec((1, c), const), pl.BlockSpec((1, c), const)],
        out_specs=pl.BlockSpec((n, c), lambda i: (i, 0)),
        out_shape=jax.ShapeDtypeStruct((t, c), BF16),
        scratch_shapes=[pltpu.VMEM((n + 2 * CONF_PAD, c), F32)],
        compiler_params=_cparams("arbitrary"),
        name="conformer_conv",
    )(z, w, b, g, beta)


def _route(sel, s):
    grp_score = []
    for g in range(N_GROUPS):
        r = sel[g * GROUP_SIZE:(g + 1) * GROUP_SIZE]
        best = None
        for i in range(GROUP_SIZE):
            for j in range(i + 1, GROUP_SIZE):
                pair = r[i] + r[j]
                best = pair if best is None else jnp.maximum(best, pair)
        grp_score.append(best)
    gbest = grp_score[0]
    gidx = jnp.zeros_like(gbest, dtype=jnp.int32)
    for g in range(1, N_GROUPS):
        better = grp_score[g] > gbest
        gidx = jnp.where(better, g, gidx)
        gbest = jnp.where(better, grp_score[g], gbest)
    neg = jnp.full_like(gbest, -jnp.inf)
    masked = [jnp.where(gidx == e // GROUP_SIZE, sel[e], neg) for e in range(N_EXPERTS)]

    def argmax_first(vals):
        best = vals[0]
        idx = jnp.zeros_like(gidx)
        for e in range(1, N_EXPERTS):
            better = vals[e] > best
            idx = jnp.where(better, e, idx)
            best = jnp.where(better, vals[e], best)
        return idx

    i1 = argmax_first(masked)
    i2 = argmax_first([jnp.where(i1 == e, neg, masked[e]) for e in range(N_EXPERTS)])
    zero = jnp.zeros_like(gbest)
    w1 = sum(jnp.where(i1 == e, s[e], zero) for e in range(N_EXPERTS))
    w2 = sum(jnp.where(i2 == e, s[e], zero) for e in range(N_EXPERTS))
    tot = w1 + w2
    return [(jnp.where(i1 == e, w1, zero) + jnp.where(i2 == e, w2, zero)) / tot
            for e in range(N_EXPERTS)]


def _merge_kernel(oa_ref, oh_ref, oc_ref, zg_ref, x_ref, mod_ref, g2_ref,
                  pa_ref, ph_ref, pc_ref, wo_ref, wr_ref, br_ref,
                  xo_ref, h2_ref, comb_ref):
    d = x_ref.shape[1]
    merged = jax.nn.sigmoid(zg_ref[:, 0:d]) * _dot(oa_ref[...], pa_ref[...])
    merged += jax.nn.sigmoid(zg_ref[:, d:2 * d]) * _dot(oh_ref[...], ph_ref[...])
    merged += jax.nn.sigmoid(zg_ref[:, 2 * d:3 * d]) * _dot(oc_ref[...], pc_ref[...])
    y = _dot(merged.astype(BF16), wo_ref[...])
    x = x_ref[...] + mod_ref[0, :, 2 * d:3 * d] * y
    xo_ref[...] = x
    ms = jnp.mean(x * x, axis=-1, keepdims=True)
    h2 = x * lax.rsqrt(ms + NORM_EPS) * g2_ref[...]
    h2 = h2 * (1.0 + mod_ref[0, :, 4 * d:5 * d]) + mod_ref[0, :, 3 * d:4 * d]
    h2_ref[...] = h2.astype(BF16)
    hh, hl = _split(h2)
    wh, wl = _split(wr_ref[...])
    nt = (((1,), (1,)), ((), ()))
    logits = (lax.dot_general(wh, hh, nt, preferred_element_type=F32)
              + (lax.dot_general(wh, hl, nt, preferred_element_type=F32)
                 + lax.dot_general(wl, hh, nt, preferred_element_type=F32)))
    s = jax.nn.sigmoid(logits)
    sel = s + br_ref[...]
    comb = _route([sel[e:e + 1, :] for e in range(N_EXPERTS)],
                  [s[e:e + 1, :] for e in range(N_EXPERTS)])
    comb_ref[...] = jnp.concatenate(comb, axis=0).T


def _merge_call(oa, oh, oc, zg, x, mod, g2, pa, ph, pc, wo, wr_t, br, *, tm):
    t, d = x.shape
    tiles_per_mod = t // mod.shape[0] // tm
    row = lambda i: (i, 0)
    const = lambda i: (0, 0)
    full = lambda a: pl.BlockSpec(a.shape, const)
    return pl.pallas_call(
        _merge_kernel,
        grid=(t // tm,),
        in_specs=[pl.BlockSpec((tm, oa.shape[1]), row), pl.BlockSpec((tm, oh.shape[1]), row),
                  pl.BlockSpec((tm, oc.shape[1]), row), pl.BlockSpec((tm, zg.shape[1]), row),
                  pl.BlockSpec((tm, d), row),
                  pl.BlockSpec((1, 1, mod.shape[2]), lambda i: (i // tiles_per_mod, 0, 0)),
                  full(g2), full(pa), full(ph), full(pc), full(wo), full(wr_t), full(br)],
        out_specs=[pl.BlockSpec((tm, d), row), pl.BlockSpec((tm, d), row),
                   pl.BlockSpec((tm, N_EXPERTS), row)],
        out_shape=[jax.ShapeDtypeStruct((t, d), F32), jax.ShapeDtypeStruct((t, d), BF16),
                   jax.ShapeDtypeStruct((t, N_EXPERTS), F32)],
        compiler_params=_cparams("arbitrary"),
        name="merge_router",
    )(oa, oh, oc, zg, x, mod, g2, pa, ph, pc, wo, wr_t, br)


def _moe_kernel(h_ref, comb_ref, wg_ref, wu_ref, wd_ref, x_ref, mod_ref, o_ref, acc_ref):
    e = pl.program_id(1)
    d = x_ref.shape[1]

    @pl.when(e == 0)
    def _():
        acc_ref[...] = jnp.zeros_like(acc_ref)

    h = h_ref[...]
    hid = _silu(_dot(h, wg_ref[0])) * _dot(h, wu_ref[0])
    comb = comb_ref[...]
    lane = lax.broadcasted_iota(jnp.int32, comb.shape, 1)
    ce = jnp.sum(jnp.where(lane == e, comb, 0.0), axis=-1, keepdims=True)
    acc_ref[...] += _dot((hid * ce).astype(BF16), wd_ref[0])

    @pl.when(e == pl.num_programs(1) - 1)
    def _():
        o_ref[...] = x_ref[...] + mod_ref[0, :, 5 * d:6 * d] * acc_ref[...]


def _moe_call(h2, comb, wg, wu, wd, x, mod, *, tm):
    t, d = x.shape
    tiles_per_mod = t // mod.shape[0] // tm
    ne, _, ff = wg.shape
    row = lambda i, e: (i, 0)
    return pl.pallas_call(
        _moe_kernel,
        grid=(t // tm, ne),
        in_specs=[pl.BlockSpec((tm, d), row), pl.BlockSpec((tm, ne), row),
                  pl.BlockSpec((1, d, ff), lambda i, e: (e, 0, 0)),
                  pl.BlockSpec((1, d, ff), lambda i, e: (e, 0, 0)),
                  pl.BlockSpec((1, ff, d), lambda i, e: (e, 0, 0)),
                  pl.BlockSpec((tm, d), row),
                  pl.BlockSpec((1, 1, mod.shape[2]), lambda i, e: (i // tiles_per_mod, 0, 0))],
        out_specs=pl.BlockSpec((tm, d), row),
        out_shape=jax.ShapeDtypeStruct((t, d), F32),
        scratch_shapes=[pltpu.VMEM((tm, d), F32)],
        compiler_params=_cparams("arbitrary", "arbitrary"),
        name="moe_experts",
    )(h2, comb, wg, wu, wd, x, mod)


def _rope_tables(n_tokens):
    rows = n_tokens // GRID_W
    row_ids = jnp.repeat(jnp.arange(rows), GRID_W).astype(F32)
    col_ids = jnp.tile(jnp.arange(GRID_W), rows).astype(F32)
    half = HEAD_DIM // 2
    inv = ROPE_THETA ** (-jnp.arange(0, half, 2, dtype=F32) / half)
    ang_r = row_ids[:, None] * inv
    ang_c = col_ids[:, None] * inv
    ang = jnp.concatenate([ang_r, ang_r, ang_c, ang_c] * (LANES // HEAD_DIM), axis=-1)
    return jnp.cos(ang), jnp.sin(ang)


def _head_mean_matrix():
    i = lax.broadcasted_iota(jnp.int32, (LANES, LANES), 0) // HEAD_DIM
    j = lax.broadcasted_iota(jnp.int32, (LANES, LANES), 1) // HEAD_DIM
    return jnp.where(i == j, 1.0 / HEAD_DIM, 0.0).astype(BF16)


def kernel(x, c, ctx, c_ctx, w_mod, b_mod, g_norm1, g_norm2, w_in, q_norm, k_norm, hy_short_w, hy_short_b, hy_w1, hy_b1, hy_w2, hy_b2, hy_w3, hy_freq, hy_bias, cf_dw_w, cf_dw_b, cf_ln_g, cf_ln_b, p_attn, p_hyena, p_conv, w_out, w_router, b_router, w_exp_gate, w_exp_up, w_exp_down):
    batch, seq, d = x.shape
    n_ctx = ctx.shape[1]
    depth = w_mod.shape[0]
    hy_w = hy_short_w.shape[2]
    cf_w = 2 * cf_dw_w.shape[2]
    tm = 512

    pad_rows = (-(batch + 1)) % 8
    cc = jnp.concatenate([c, c_ctx[None, :], jnp.zeros((pad_rows, d), F32)], axis=0)
    mods = _mod_call(cc, w_mod, b_mod)

    cos_x, sin_x = _rope_tables(seq)
    cos_c = jnp.ones((tm, LANES), F32)
    sin_c = jnp.zeros((tm, LANES), F32)
    hm = _head_mean_matrix()
    tile2 = lambda g: jnp.tile(g, (1, LANES // HEAD_DIM))
    qg, kg = tile2(q_norm), tile2(k_norm)

    w1p = jnp.pad(hy_w1, ((0, 0), (0, LANES - hy_w1.shape[1]), (0, 0)))
    r3 = lambda a: a[:, None, :]
    tabs_x = _dft_tables(seq)
    tabs_c = _dft_tables(n_ctx)
    filt_args = (w1p, r3(hy_b1), hy_w2, r3(hy_b2), hy_w3, r3(hy_freq))
    kr_x, ki_x, kn_x = _hyena_filter_call(seq, tabs_x, *filt_args)
    kr_c, ki_c, kn_c = _hyena_filter_call(n_ctx, tabs_c, *filt_args)

    bf = lambda a: a.astype(BF16)
    w_in_b, pa_b, ph_b, pc_b, wo_b = bf(w_in), bf(p_attn), bf(p_hyena), bf(p_conv), bf(w_out)
    wg_b, wu_b, wd_b = bf(w_exp_gate), bf(w_exp_up), bf(w_exp_down)
    wr_t = w_router.T
    br = b_router[:, None]

    xl = x.reshape(batch * seq, d)
    xc = ctx.reshape(batch * n_ctx, d)
    for l in range(depth):
        last = l == depth - 1
        mod_x = mods[l, :batch][:, None, :]
        mod_c = mods[l, batch:batch + 1][:, None, :]
        row = lambda a: a[l][None, :]
        streams = [(xl, mod_x, seq, cos_x, sin_x, tabs_x, kr_x, ki_x, kn_x),
                   (xc, mod_c, n_ctx, cos_c, sin_c, tabs_c, kr_c, ki_c, kn_c)]
        proj = [
            _inproj_call(s[0], s[1], row(g_norm1), w_in_b[l], s[3], s[4], qg[l:l + 1], kg[l:l + 1], hm,
                         tm=tm, hy_w=hy_w, cf_w=cf_w)
            for s in streams]
        (q_x, k_x, v_x, zhy_x, zcf_x, zg_x), (q_c, k_c, v_c, zhy_c, zcf_c, zg_c) = proj
        new = []
        for si, s in enumerate(streams):
            if si == 1 and last:
                break
            xs, mod, n, _, _, tabs, kr, ki, kn = s
            if si == 0:
                oa = _attn_call(q_x, [(k_x, v_x), (k_c, v_c)], batch=batch, tq=256)
                zhy, zcf, zg = zhy_x, zcf_x, zg_x
            else:
                oa = _attn_call(q_c, [(k_c, v_c)], batch=batch, tq=n_ctx)
                zhy, zcf, zg = zhy_c, zcf_c, zg_c
            oh = _hyena_call(zhy, hy_short_w[l], row(hy_short_b), tabs[0], tabs[2], kr[l], ki[l], kn[l],
                             row(hy_bias), n=n)
            oc = _conf_call(zcf, cf_dw_w[l], row(cf_dw_b), row(cf_ln_g), row(cf_ln_b), n=n)
            x1, h2, comb = _merge_call(oa, oh, oc, zg, xs, mod, row(g_norm2), pa_b[l], ph_b[l], pc_b[l],
                                       wo_b[l], wr_t, br, tm=tm)
            new.append(_moe_call(h2, comb, wg_b[l], wu_b[l], wd_b[l], x1, mod, tm=2 * tm))
        xl = new[0]
        if not last:
            xc = new[1]
    return xl.reshape(batch, seq, d)
```

```python
import functools
import math

import jax
import jax.numpy as jnp
from jax import lax
from jax.experimental import pallas as pl
from jax.experimental.pallas import tpu as pltpu

F32 = jnp.float32
BF16 = jnp.bfloat16

NORM_EPS = 1e-6
N_MOD = 6
HEAD_DIM = 64
N_Q_HEADS = 8
N_KV_HEADS = 2
ATTN_W = N_Q_HEADS * HEAD_DIM
KV_W = N_KV_HEADS * HEAD_DIM
GRID_W = 64
ROPE_THETA = 10000.0
HYENA_BANDS = 16
HYENA_FAST_DECAY = 0.3
HYENA_SLOW_DECAY = 1.5
HYENA_TARGET = 1e-2
CONF_K = 31
N_EXPERTS = 16
N_GROUPS = 4
GROUP_SIZE = N_EXPERTS // N_GROUPS

LANES = 128
VMEM_LIMIT = 56 * 1024 * 1024


def _cparams(*sem):
    return pltpu.CompilerParams(dimension_semantics=sem, vmem_limit_bytes=VMEM_LIMIT)


def _dot(a, b):
    return jnp.dot(a, b, preferred_element_type=F32)


def _split(a):
    hi = a.astype(BF16)
    lo = (a - hi.astype(F32)).astype(BF16)
    return hi, lo


def _dot3(a, b):
    ah, al = _split(a)
    bh, bl = _split(b)
    return _dot(ah, bh) + (_dot(al, bh) + _dot(ah, bl))


def _silu(x):
    return x * jax.nn.sigmoid(x)


def _mod_kernel(c_ref, w_ref, b_ref, o_ref):
    o_ref[0] = _dot3(_silu(c_ref[...]), w_ref[0]) + b_ref[0]


def _mod_call(cc, w_mod, b_mod):
    depth, d, n = w_mod.shape
    r = cc.shape[0]
    tn = 1536
    return pl.pallas_call(
        _mod_kernel,
        grid=(depth, n // tn),
        in_specs=[pl.BlockSpec((r, d), lambda l, j: (0, 0)),
                  pl.BlockSpec((1, d, tn), lambda l, j: (l, 0, j)),
                  pl.BlockSpec((1, 1, tn), lambda l, j: (l, 0, j))],
        out_specs=pl.BlockSpec((1, r, tn), lambda l, j: (l, 0, j)),
        out_shape=jax.ShapeDtypeStruct((depth, r, n), F32),
        compiler_params=_cparams("arbitrary", "arbitrary"),
        name="adaln_mod",
    )(cc, w_mod, b_mod.reshape(depth, 1, n))


def _inproj_kernel(x_ref, mod_ref, g_ref, w_ref, cos_ref, sin_ref, qg_ref, kg_ref, hm_ref,
                   q_ref, k_ref, v_ref, zhy_ref, zcf_ref, zg_ref):
    d = x_ref.shape[1]
    x = x_ref[...]
    ms = jnp.mean(x * x, axis=-1, keepdims=True)
    y = x * lax.rsqrt(ms + NORM_EPS) * g_ref[...]
    sh = mod_ref[0, :, 0:d]
    sc = mod_ref[0, :, d:2 * d]
    h = (y * (1.0 + sc) + sh).astype(BF16)

    cos = cos_ref[...]
    sin = sin_ref[...]
    hm = hm_ref[...]
    lane = lax.broadcasted_iota(jnp.int32, (1, LANES), 1)
    first_quarter = (lane % 32) < 16

    def norm_rope(z, gain, scale):
        hi, lo = _split(z * z)
        msq = _dot(hi, hm) + _dot(lo, hm)
        zn = z * lax.rsqrt(msq + NORM_EPS) * gain
        rot = jnp.where(first_quarter, -pltpu.roll(zn, LANES - 16, 1), pltpu.roll(zn, 16, 1))
        return (zn * cos + rot * sin) * scale

    zq = _dot(h, w_ref[:, 0:ATTN_W])
    for cb in range(ATTN_W // LANES):
        sl = slice(cb * LANES, (cb + 1) * LANES)
        q_ref[:, sl] = norm_rope(zq[:, sl], qg_ref[...], 1.0 / math.sqrt(HEAD_DIM)).astype(BF16)
    zkv = _dot(h, w_ref[:, ATTN_W:ATTN_W + 2 * KV_W])
    k_ref[...] = norm_rope(zkv[:, 0:KV_W], kg_ref[...], 1.0).astype(BF16)
    v_ref[...] = zkv[:, KV_W:2 * KV_W].astype(BF16)
    off = ATTN_W + 2 * KV_W
    for ref in (zhy_ref, zcf_ref):
        w = ref.shape[1]
        ref[...] = _dot(h, w_ref[:, off:off + w])
        off += w
    chunk = 1024
    for cb in range(zg_ref.shape[1] // chunk):
        zg_ref[:, cb * chunk:(cb + 1) * chunk] = _dot(
            h, w_ref[:, off + cb * chunk:off + (cb + 1) * chunk])


def _inproj_call(x, mod, g1, w_in, cos, sin, qg, kg, hm, *, tm, hy_w, cf_w):
    t, d = x.shape
    tiles_per_mod = t // mod.shape[0] // tm
    n = w_in.shape[1]
    g_w = n - ATTN_W - 2 * KV_W - hy_w - cf_w
    n_pos = cos.shape[0] // tm
    row = lambda i: (i, 0)
    const = lambda i: (0, 0)
    return pl.pallas_call(
        _inproj_kernel,
        grid=(t // tm,),
        in_specs=[pl.BlockSpec((tm, d), row),
                  pl.BlockSpec((1, 1, mod.shape[2]), lambda i: (i // tiles_per_mod, 0, 0)),
                  pl.BlockSpec((1, d), const),
                  pl.BlockSpec((d, n), const),
                  pl.BlockSpec((tm, LANES), lambda i: (i % n_pos, 0)),
                  pl.BlockSpec((tm, LANES), lambda i: (i % n_pos, 0)),
                  pl.BlockSpec((1, LANES), const),
                  pl.BlockSpec((1, LANES), const),
                  pl.BlockSpec((LANES, LANES), const)],
        out_specs=[pl.BlockSpec((tm, ATTN_W), row),
                   pl.BlockSpec((tm, KV_W), row),
                   pl.BlockSpec((tm, KV_W), row),
                   pl.BlockSpec((tm, hy_w), row),
                   pl.BlockSpec((tm, cf_w), row),
                   pl.BlockSpec((tm, g_w), row)],
        out_shape=[jax.ShapeDtypeStruct((t, ATTN_W), BF16),
                   jax.ShapeDtypeStruct((t, KV_W), BF16),
                   jax.ShapeDtypeStruct((t, KV_W), BF16),
                   jax.ShapeDtypeStruct((t, hy_w), F32),
                   jax.ShapeDtypeStruct((t, cf_w), F32),
                   jax.ShapeDtypeStruct((t, g_w), F32)],
        compiler_params=_cparams("arbitrary"),
        name="in_proj",
    )(x, mod, g1, w_in, cos, sin, qg, kg, hm)


def _attn_kernel(*refs, n_src):
    q_ref = refs[0]
    kv_refs = refs[1:1 + 2 * n_src]
    o_ref = refs[1 + 2 * n_src]
    kk_ref, vv_ref = refs[2 + 2 * n_src:]
    h = pl.program_id(1)
    lane = lax.broadcasted_iota(jnp.int32, (1, LANES), 1)
    lo_half = lane < HEAD_DIM

    @pl.when(pl.program_id(2) == 0)
    def _():
        keep = (lane // HEAD_DIM) == h
        off = 0
        for s in range(n_src):
            n = kv_refs[2 * s].shape[0]
            for src, dst in ((kv_refs[2 * s], kk_ref), (kv_refs[2 * s + 1], vv_ref)):
                a = src[...].astype(F32)
                dst[off:off + n, :] = jnp.where(keep, a, pltpu.roll(a, HEAD_DIM, 1)).astype(BF16)
            off += n

    kk = kk_ref[...]
    vv = vv_ref[...]
    for j in range(q_ref.shape[1] // LANES):
        q2 = q_ref[:, j * LANES:(j + 1) * LANES]
        outs = []
        for half in range(2):
            msk = lo_half if half == 0 else jnp.logical_not(lo_half)
            qm = jnp.where(msk, q2, jnp.zeros_like(q2))
            s = lax.dot_general(qm, kk, (((1,), (1,)), ((), ())), preferred_element_type=F32)
            m = jnp.max(s, axis=-1, keepdims=True)
            p = jnp.exp(s - m)
            l = jnp.sum(p, axis=-1, keepdims=True)
            outs.append(_dot(p.astype(BF16), vv) / l)
        o_ref[:, j * LANES:(j + 1) * LANES] = jnp.where(lo_half, outs[0], outs[1]).astype(BF16)


def _attn_call(q, kvs, *, batch, tq):
    t = q.shape[0]
    nq = t // batch // tq
    gw = ATTN_W // N_KV_HEADS
    in_specs = [pl.BlockSpec((tq, gw), lambda b, h, i: (b * nq + i, h))]
    args = [q]
    nk = 0
    for k, v in kvs:
        n = k.shape[0] // batch
        nk += n
        for a in (k, v):
            in_specs.append(pl.BlockSpec((n, KV_W), lambda b, h, i: (b, 0)))
            args.append(a)
    return pl.pallas_call(
        functools.partial(_attn_kernel, n_src=len(kvs)),
        grid=(batch, N_KV_HEADS, nq),
        in_specs=in_specs,
        out_specs=pl.BlockSpec((tq, gw), lambda b, h, i: (b * nq + i, h)),
        out_shape=jax.ShapeDtypeStruct((t, ATTN_W), BF16),
        scratch_shapes=[pltpu.VMEM((nk, LANES), BF16), pltpu.VMEM((nk, LANES), BF16)],
        compiler_params=_cparams("arbitrary", "arbitrary", "arbitrary"),
        name="attention",
    )(*args)


def _hy_time_kernel(feat_ref, w1_ref, b1_ref, w2_ref, b2_ref, w3_ref, fq_ref, dec_ref,
                    sp_ref, sm_ref, kn_ref):
    n = feat_ref.shape[0]
    fq = fq_ref[0]
    h = jnp.sin(fq * (_dot3(feat_ref[...], w1_ref[0]) + b1_ref[0]))
    h = jnp.sin(fq * (_dot3(h, w2_ref[0]) + b2_ref[0]))
    h = _dot3(h, w3_ref[0]) * dec_ref[...]
    h = h / jnp.sum(jnp.abs(h), axis=0, keepdims=True)
    c = h.shape[1] // 2
    row = lax.broadcasted_iota(jnp.int32, (n, 1), 0)
    hf = h[:, 0:c]
    hb = jnp.where(row == 0, 0.0, h[:, c:2 * c])
    sp = hf + hb
    sp_ref[0] = sp
    sm_ref[0] = hf - hb
    sgn = (1 - 2 * (row & 1)).astype(F32)
    kn_ref[0] = jnp.sum(sp * sgn, axis=0, keepdims=True) * (0.5 / n)


def _hy_spec_kernel(fch_ref, fcl_ref, fsh_ref, fsl_ref, sp_ref, sm_ref, kr_ref, ki_ref, *, n):
    tk = fch_ref.shape[0]
    k = pl.program_id(1) * tk + lax.broadcasted_iota(jnp.int32, (tk, 1), 0)
    scl = jnp.where(k == 0, 0.5 / n, 1.0 / n)
    sph, spl = _split(sp_ref[0])
    smh, sml = _split(sm_ref[0])
    kr = _dot(fch_ref[...], sph) + (_dot(fcl_ref[...], sph) + _dot(fch_ref[...], spl))
    ki = _dot(fsh_ref[...], smh) + (_dot(fsl_ref[...], smh) + _dot(fsh_ref[...], sml))
    kr_ref[0] = kr * scl
    ki_ref[0] = ki * scl


def _dft_tables(n):
    k = lax.broadcasted_iota(jnp.int32, (n, n), 0)
    t = lax.broadcasted_iota(jnp.int32, (n, n), 1)
    ang = ((k * t) % (2 * n)).astype(F32) * (math.pi / n)
    return _split(jnp.cos(ang)) + _split(-jnp.sin(ang))


def _hyena_features(n, c):
    t = jnp.linspace(0.0, 1.0, n, dtype=F32)[:, None]
    w = 2.0 * math.pi * jnp.arange(n, dtype=F32)[:, None] / n
    f = jnp.linspace(1e-4, HYENA_BANDS - 1, HYENA_BANDS, dtype=F32)[None, :]
    feat = jnp.concatenate([t, jnp.cos(f * w), -jnp.sin(f * w)], axis=-1)
    feat = jnp.pad(feat, ((0, 0), (0, LANES - feat.shape[1])))
    max_decay = math.log(HYENA_TARGET) / HYENA_FAST_DECAY
    min_decay = math.log(HYENA_TARGET) / HYENA_SLOW_DECAY
    deltas = jnp.tile(jnp.linspace(min_decay, max_decay, c, dtype=F32), 2)
    return feat, jnp.exp(-t * jnp.abs(deltas))


def _hyena_filter_call(n, tabs, w1p, b1, w2, b2, w3, fq):
    depth = w1p.shape[0]
    ffn = w2.shape[1]
    c = w3.shape[2] // 2
    feat, dec = _hyena_features(n, c)
    lay = lambda l: (l, 0, 0)
    const = lambda l: (0, 0)
    sp, sm, kn = pl.pallas_call(
        _hy_time_kernel,
        grid=(depth,),
        in_specs=[pl.BlockSpec((n, LANES), const),
                  pl.BlockSpec((1, LANES, ffn), lay), pl.BlockSpec((1, 1, ffn), lay),
                  pl.BlockSpec((1, ffn, ffn), lay), pl.BlockSpec((1, 1, ffn), lay),
                  pl.BlockSpec((1, ffn, 2 * c), lay), pl.BlockSpec((1, 1, ffn), lay),
                  pl.BlockSpec((n, 2 * c), const)],
        out_specs=[pl.BlockSpec((1, n, c), lay), pl.BlockSpec((1, n, c), lay),
                   pl.BlockSpec((1, 1, c), lay)],
        out_shape=[jax.ShapeDtypeStruct((depth, n, c), F32), jax.ShapeDtypeStruct((depth, n, c), F32),
                   jax.ShapeDtypeStruct((depth, 1, c), F32)],
        compiler_params=_cparams("arbitrary"),
        name="hyena_filter_time",
    )(feat, w1p, b1, w2, b2, w3, fq, dec)
    tk = min(n, 512)
    mat = pl.BlockSpec((tk, n), lambda l, j: (j, 0))
    vec = pl.BlockSpec((1, n, c), lambda l, j: (l, 0, 0))
    out = pl.BlockSpec((1, tk, c), lambda l, j: (l, j, 0))
    kr, ki = pl.pallas_call(
        functools.partial(_hy_spec_kernel, n=n),
        grid=(depth, n // tk),
        in_specs=[mat, mat, mat, mat, vec, vec],
        out_specs=[out, out],
        out_shape=[jax.ShapeDtypeStruct((depth, n, c), F32)] * 2,
        compiler_params=_cparams("arbitrary", "arbitrary"),
        name="hyena_filter_spectrum",
    )(*tabs, sp, sm)
    return kr, ki, kn


def _hyena_kernel(z_ref, sw_ref, sb_ref, fc_ref, fs_ref, kr_ref, ki_ref, kn_ref, db_ref, o_ref):
    n = z_ref.shape[0]
    c = o_ref.shape[1]
    z = z_ref[...]
    row = lax.broadcasted_iota(jnp.int32, (n, 1), 0)
    z_prev = jnp.where(row == 0, 0.0, pltpu.roll(z, 1, 0))
    z_next = jnp.where(row == n - 1, 0.0, pltpu.roll(z, n - 1, 0))
    zc = z_prev * sw_ref[0:1, :] + z * sw_ref[1:2, :] + z_next * sw_ref[2:3, :] + sb_ref[...]
    x0 = zc[:, 0:c]
    u = zc[:, 2 * c:3 * c] * zc[:, c:2 * c]
    ub = u.astype(BF16)
    fc = fc_ref[...]
    fs = fs_ref[...]
    ur = _dot(fc, ub)
    ui = _dot(fs, ub)
    kr = kr_ref[...]
    ki = ki_ref[...]
    pr = (ur * kr - ui * ki).astype(BF16)
    pi = (ur * ki + ui * kr).astype(BF16)
    sgn = (1 - 2 * (row & 1)).astype(F32)
    nyq = jnp.sum(u * sgn, axis=0, keepdims=True) * kn_ref[...]
    y = _dot(fc, pr) + _dot(fs, pi) + sgn * nyq
    o_ref[...] = ((y + u * db_ref[...]) * x0).astype(BF16)


def _hyena_call(z, sw, sb, fc, fs, kr, ki, kn, db, *, n):
    t, w = z.shape
    c = w // 3
    const = lambda b: (0, 0)
    return pl.pallas_call(
        _hyena_kernel,
        grid=(t // n,),
        in_specs=[pl.BlockSpec((n, w), lambda b: (b, 0)),
                  pl.BlockSpec((3, w), const), pl.BlockSpec((1, w), const),
                  pl.BlockSpec((n, n), const), pl.BlockSpec((n, n), const),
                  pl.BlockSpec((n, c), const), pl.BlockSpec((n, c), const),
                  pl.BlockSpec((1, c), const), pl.BlockSpec((1, c), const)],
        out_specs=pl.BlockSpec((n, c), lambda b: (b, 0)),
        out_shape=jax.ShapeDtypeStruct((t, c), BF16),
        compiler_params=_cparams("arbitrary"),
        name="hyena_conv",
    )(z, sw, sb, fc, fs, kr, ki, kn, db)


CONF_PAD = 16


def _conf_kernel(z_ref, w_ref, b_ref, g_ref, beta_ref, o_ref, pad_ref):
    n = z_ref.shape[0]
    c = o_ref.shape[1]
    u = z_ref[:, 0:c] * jax.nn.sigmoid(z_ref[:, c:2 * c])
    zeros = jnp.zeros((CONF_PAD, c), F32)
    pad_ref[0:CONF_PAD, :] = zeros
    pad_ref[CONF_PAD + n:2 * CONF_PAD + n, :] = zeros
    pad_ref[CONF_PAD:CONF_PAD + n, :] = u
    first = CONF_PAD - (CONF_K - 1) // 2
    acc = jnp.zeros((n, c), F32) + b_ref[...]
    for j in range(CONF_K):
        acc = acc + w_ref[j:j + 1, :] * pad_ref[first + j:first + j + n, :]
    mu = jnp.mean(acc, axis=-1, keepdims=True)
    dlt = acc - mu
    var = jnp.mean(dlt * dlt, axis=-1, keepdims=True)
    y = dlt * lax.rsqrt(var + NORM_EPS) * g_ref[...] + beta_ref[...]
    o_ref[...] = _silu(y).astype(BF16)


def _conf_call(z, w, b, g, beta, *, n):
    t, w2 = z.shape
    c = w2 // 2
    const = lambda i: (0, 0)
    return pl.pallas_call(
        _conf_kernel,
        grid=(t // n,),
        in_specs=[pl.BlockSpec((n, w2), lambda i: (i, 0)),
                  pl.BlockSpec((CONF_K, c), const), pl.BlockSpec((1, c), const),
                  pl.BlockSpec((1, c), const), pl.BlockSpec((1, c), const)],
        out_specs=pl.BlockSpec((n, c), lambda i: (i, 0)),
        out_shape=jax.ShapeDtypeStruct((t, c), BF16),
        scratch_shapes=[pltpu.VMEM((n + 2 * CONF_PAD, c), F32)],
        compiler_params=_cparams("arbitrary"),
        name="conformer_conv",
    )(z, w, b, g, beta)


def _route(sel, s):
    grp_score = []
    for g in range(N_GROUPS):
        r = sel[g * GROUP_SIZE:(g + 1) * GROUP_SIZE]
        best = None
        for i in range(GROUP_SIZE):
            for j in range(i + 1, GROUP_SIZE):
                pair = r[i] + r[j]
                best = pair if best is None else jnp.maximum(best, pair)
        grp_score.append(best)
    gbest = grp_score[0]
    gidx = jnp.zeros_like(gbest, dtype=jnp.int32)
    for g in range(1, N_GROUPS):
        better = grp_score[g] > gbest
        gidx = jnp.where(better, g, gidx)
        gbest = jnp.where(better, grp_score[g], gbest)
    neg = jnp.full_like(gbest, -jnp.inf)
    masked = [jnp.where(gidx == e // GROUP_SIZE, sel[e], neg) for e in range(N_EXPERTS)]

    def argmax_first(vals):
        best = vals[0]
        idx = jnp.zeros_like(gidx)
        for e in range(1, N_EXPERTS):
            better = vals[e] > best
            idx = jnp.where(better, e, idx)
            best = jnp.where(better, vals[e], best)
        return idx

    i1 = argmax_first(masked)
    i2 = argmax_first([jnp.where(i1 == e, neg, masked[e]) for e in range(N_EXPERTS)])
    zero = jnp.zeros_like(gbest)
    w1 = sum(jnp.where(i1 == e, s[e], zero) for e in range(N_EXPERTS))
    w2 = sum(jnp.where(i2 == e, s[e], zero) for e in range(N_EXPERTS))
    tot = w1 + w2
    w1, w2 = w1 / tot, w2 / tot
    swap = i2 < i1
    a = jnp.where(swap, i2, i1) - GROUP_SIZE * gidx
    b = jnp.where(swap, i1, i2) - GROUP_SIZE * gidx
    cls = PAIRS_PER_GROUP * gidx + ((a * (2 * GROUP_SIZE - 1 - a)) >> 1) + (b - a - 1)
    return cls, jnp.where(swap, w2, w1), jnp.where(swap, w1, w2)


def _merge_kernel(oa_ref, oh_ref, oc_ref, zg_ref, x_ref, mod_ref, g2_ref,
                  pa_ref, ph_ref, pc_ref, wo_ref, wr_ref, br_ref,
                  xo_ref, hx_ref, cls_ref):
    d = x_ref.shape[1]
    tm = x_ref.shape[0]
    merged = jax.nn.sigmoid(zg_ref[:, 0:d]) * _dot(oa_ref[...], pa_ref[...])
    merged += jax.nn.sigmoid(zg_ref[:, d:2 * d]) * _dot(oh_ref[...], ph_ref[...])
    merged += jax.nn.sigmoid(zg_ref[:, 2 * d:3 * d]) * _dot(oc_ref[...], pc_ref[...])
    y = _dot(merged.astype(BF16), wo_ref[...])
    x = x_ref[...] + mod_ref[0, :, 2 * d:3 * d] * y
    xo_ref[...] = x
    ms = jnp.mean(x * x, axis=-1, keepdims=True)
    h2 = x * lax.rsqrt(ms + NORM_EPS) * g2_ref[...]
    h2 = h2 * (1.0 + mod_ref[0, :, 4 * d:5 * d]) + mod_ref[0, :, 3 * d:4 * d]
    hx_ref[:, 0:d] = h2
    hh, hl = _split(h2)
    wh, wl = _split(wr_ref[...])
    nt = (((1,), (1,)), ((), ()))
    logits = (lax.dot_general(wh, hh, nt, preferred_element_type=F32)
              + (lax.dot_general(wh, hl, nt, preferred_element_type=F32)
                 + lax.dot_general(wl, hh, nt, preferred_element_type=F32)))
    s = jax.nn.sigmoid(logits)
    sel = s + br_ref[...]
    cls, w_lo, w_hi = _route([sel[e:e + 1, :] for e in range(N_EXPERTS)],
                             [s[e:e + 1, :] for e in range(N_EXPERTS)])
    cls_ref[0] = cls
    sub = lax.broadcasted_iota(jnp.int32, (LANES, tm), 0)
    rows = jnp.where(sub == 0, w_lo, jnp.where(sub == 1, w_hi, 0.0))
    hx_ref[:, d:d + LANES] = rows.T


def _merge_call(oa, oh, oc, zg, x, mod, g2, pa, ph, pc, wo, wr_t, br, *, tm):
    t, d = x.shape
    tiles_per_mod = t // mod.shape[0] // tm
    row = lambda i: (i, 0)
    const = lambda i: (0, 0)
    full = lambda a: pl.BlockSpec(a.shape, const)
    return pl.pallas_call(
        _merge_kernel,
        grid=(t // tm,),
        in_specs=[pl.BlockSpec((tm, oa.shape[1]), row), pl.BlockSpec((tm, oh.shape[1]), row),
                  pl.BlockSpec((tm, oc.shape[1]), row), pl.BlockSpec((tm, zg.shape[1]), row),
                  pl.BlockSpec((tm, d), row),
                  pl.BlockSpec((1, 1, mod.shape[2]), lambda i: (i // tiles_per_mod, 0, 0)),
                  full(g2), full(pa), full(ph), full(pc), full(wo), full(wr_t), full(br)],
        out_specs=[pl.BlockSpec((tm, d), row), pl.BlockSpec((tm, d + LANES), row),
                   pl.BlockSpec((1, 1, tm), lambda i: (i, 0, 0))],
        out_shape=[jax.ShapeDtypeStruct((t, d), F32), jax.ShapeDtypeStruct((t, d + LANES), F32),
                   jax.ShapeDtypeStruct((t // tm, 1, tm), jnp.int32)],
        compiler_params=_cparams("arbitrary"),
        name="merge_router",
    )(oa, oh, oc, zg, x, mod, g2, pa, ph, pc, wo, wr_t, br)


PAIRS_PER_GROUP = GROUP_SIZE * (GROUP_SIZE - 1) // 2
N_CLASSES = N_GROUPS * PAIRS_PER_GROUP
SORT_TILE = 512


def _class_experts():
    lo, hi = [], []
    for g in range(N_GROUPS):
        for a in range(GROUP_SIZE):
            for b in range(a + 1, GROUP_SIZE):
                lo.append(g * GROUP_SIZE + a)
                hi.append(g * GROUP_SIZE + b)
    return jnp.asarray(lo + hi, dtype=jnp.int32)


def _col(row):
    return jnp.broadcast_to(row, (LANES, row.shape[1])).T[:, 0:1]


def _row(col):
    return jnp.broadcast_to(col, (col.shape[0], LANES)).T[0:1, :]


def _rank_kernel(cls_ref, pos_ref, tcls_ref, cnt_ref, base_ref):
    phase = pl.program_id(0)
    i = pl.program_id(1)
    tm = cls_ref.shape[2]
    cls_col = _col(cls_ref[0].astype(F32)).astype(jnp.int32)
    lane = lax.broadcasted_iota(jnp.int32, (1, LANES), 1)
    onehot = (cls_col == lane).astype(F32)
    counts = jnp.sum(onehot, axis=0, keepdims=True)

    @pl.when(phase == 0)
    def _():
        @pl.when(i == 0)
        def _():
            cnt_ref[...] = jnp.zeros_like(cnt_ref)
        cnt_ref[...] += counts

    @pl.when(phase == 1)
    def _():
        @pl.when(i == 0)
        def _():
            shift = SORT_TILE.bit_length() - 1
            ntile = lax.shift_right_logical(cnt_ref[...].astype(jnp.int32) + (SORT_TILE - 1), shift)
            ntile = ntile.astype(F32)
            r = lax.broadcasted_iota(jnp.int32, (LANES, LANES), 0)
            c = lax.broadcasted_iota(jnp.int32, (LANES, LANES), 1)
            start = _dot3(jnp.broadcast_to(ntile, (8, LANES)), (r < c).astype(F32))[0:1, :]
            base_ref[...] = start * SORT_TILE
            end_col = _col(start + ntile)
            tile = lax.broadcasted_iota(jnp.int32, (1, tcls_ref.shape[1]), 1).astype(F32)
            tcls_ref[...] = jnp.sum((end_col <= tile).astype(F32), axis=0,
                                    keepdims=True).astype(jnp.int32)
        t = lax.broadcasted_iota(jnp.int32, (tm, tm), 0)
        s = lax.broadcasted_iota(jnp.int32, (tm, tm), 1)
        earlier = _dot((s < t).astype(BF16), onehot.astype(BF16))
        pos_col = jnp.sum(onehot * (earlier + base_ref[...]), axis=1, keepdims=True)
        pos_ref[0] = _row(pos_col).astype(jnp.int32)
        base_ref[...] += counts


def _rank_call(cls, n_sorted_tiles):
    nt, _, tm = cls.shape
    nsp = -(-n_sorted_tiles // LANES) * LANES
    return pl.pallas_call(
        _rank_kernel,
        grid=(2, nt),
        in_specs=[pl.BlockSpec((1, 1, tm), lambda p, i: (i, 0, 0))],
        out_specs=[pl.BlockSpec((1, 1, tm), lambda p, i: (i * p, 0, 0)),
                   pl.BlockSpec((1, nsp), lambda p, i: (0, 0))],
        out_shape=[jax.ShapeDtypeStruct((nt, 1, tm), jnp.int32),
                   jax.ShapeDtypeStruct((1, nsp), jnp.int32)],
        scratch_shapes=[pltpu.VMEM((1, LANES), F32), pltpu.VMEM((1, LANES), F32)],
        compiler_params=_cparams("arbitrary", "arbitrary"),
        name="moe_rank",
    )(cls)


def _load_tile_indices(pos_hbm, idx_smem, sem):
    cp = pltpu.make_async_copy(pos_hbm.at[pl.program_id(0)], idx_smem, sem)
    cp.start()
    cp.wait()


def _for_each_row(n, copy_of_row):
    def issue(r, carry):
        copy_of_row(r).start()
        return carry

    def drain(r, carry):
        copy_of_row(r).wait()
        return carry

    lax.fori_loop(0, n, issue, 0, unroll=8)
    lax.fori_loop(0, n, drain, 0, unroll=8)


def _dispatch_kernel(pos_hbm, hx_ref, buf_in, buf_out, idx_smem, sem_idx, sem):
    del buf_in
    _load_tile_indices(pos_hbm, idx_smem, sem_idx)
    _for_each_row(hx_ref.shape[0], lambda r: pltpu.make_async_copy(
        hx_ref.at[pl.ds(r, 1)], buf_out.at[pl.ds(idx_smem[r], 1)], sem))


def _dispatch_call(pos, hx, buf):
    t, w = hx.shape
    nt, tm = pos.shape
    any_spec = pl.BlockSpec(memory_space=pl.ANY)
    return pl.pallas_call(
        _dispatch_kernel,
        grid=(nt,),
        in_specs=[any_spec, pl.BlockSpec((tm, w), lambda i: (i, 0)), any_spec],
        out_specs=any_spec,
        out_shape=jax.ShapeDtypeStruct(buf.shape, buf.dtype),
        input_output_aliases={2: 0},
        scratch_shapes=[pltpu.SMEM((tm,), jnp.int32), pltpu.SemaphoreType.DMA, pltpu.SemaphoreType.DMA],
        compiler_params=_cparams("arbitrary"),
        name="moe_dispatch",
    )(pos, hx, buf)


def _ffn_kernel(tcls_ref, etab_ref, xs_ref, wga_ref, wua_ref, wda_ref, wgb_ref, wub_ref, wdb_ref, o_ref):
    del etab_ref
    d = o_ref.shape[1]
    valid = tcls_ref[pl.program_id(0)] < N_CLASSES

    @pl.when(valid)
    def _():
        x = xs_ref[:, 0:d].astype(BF16)
        ha = _silu(_dot(x, wga_ref[0])) * _dot(x, wua_ref[0]) * xs_ref[:, d:d + 1]
        hb = _silu(_dot(x, wgb_ref[0])) * _dot(x, wub_ref[0]) * xs_ref[:, d + 1:d + 2]
        o_ref[...] = _dot(ha.astype(BF16), wda_ref[0]) + _dot(hb.astype(BF16), wdb_ref[0])

    @pl.when(jnp.logical_not(valid))
    def _():
        o_ref[...] = jnp.zeros_like(o_ref)


def _ffn_call(tcls, etab, buf, wg, wu, wd):
    rows, w = buf.shape
    ne, d, ff = wg.shape

    def expert(which):
        def index(j, tcls_ref, etab_ref):
            c = jnp.minimum(tcls_ref[j], N_CLASSES - 1)
            return (etab_ref[which * N_CLASSES + c], 0, 0)
        return index

    up = lambda which: pl.BlockSpec((1, d, ff), expert(which))
    down = lambda which: pl.BlockSpec((1, ff, d), expert(which))
    return pl.pallas_call(
        _ffn_kernel,
        grid_spec=pltpu.PrefetchScalarGridSpec(
            num_scalar_prefetch=2,
            grid=(rows // SORT_TILE,),
            in_specs=[pl.BlockSpec((SORT_TILE, w), lambda j, *_: (j, 0)),
                      up(0), up(0), down(0), up(1), up(1), down(1)],
            out_specs=pl.BlockSpec((SORT_TILE, d), lambda j, *_: (j, 0))),
        out_shape=jax.ShapeDtypeStruct((rows, d), F32),
        compiler_params=_cparams("arbitrary"),
        name="moe_ffn",
    )(tcls, etab, buf, wg, wu, wd, wg, wu, wd)


def _combine_kernel(pos_hbm, x_ref, mod_ref, ys_hbm, o_ref, ybuf, idx_smem, sem_idx, sem):
    d = x_ref.shape[1]
    _load_tile_indices(pos_hbm, idx_smem, sem_idx)
    _for_each_row(x_ref.shape[0], lambda r: pltpu.make_async_copy(
        ys_hbm.at[pl.ds(idx_smem[r], 1)], ybuf.at[pl.ds(r, 1)], sem))
    o_ref[...] = x_ref[...] + mod_ref[0, :, 5 * d:6 * d] * ybuf[...]


def _combine_call(pos, x, mod, ys):
    t, d = x.shape
    nt, tm = pos.shape
    tiles_per_mod = t // mod.shape[0] // tm
    any_spec = pl.BlockSpec(memory_space=pl.ANY)
    return pl.pallas_call(
        _combine_kernel,
        grid=(nt,),
        in_specs=[any_spec, pl.BlockSpec((tm, d), lambda i: (i, 0)),
                  pl.BlockSpec((1, 1, mod.shape[2]), lambda i: (i // tiles_per_mod, 0, 0)), any_spec],
        out_specs=pl.BlockSpec((tm, d), lambda i: (i, 0)),
        out_shape=jax.ShapeDtypeStruct((t, d), F32),
        scratch_shapes=[pltpu.VMEM((tm, d), F32), pltpu.SMEM((tm,), jnp.int32),
                        pltpu.SemaphoreType.DMA, pltpu.SemaphoreType.DMA],
        compiler_params=_cparams("arbitrary"),
        name="moe_combine",
    )(pos, x, mod, ys)


def _moe(streams, wg, wu, wd):
    cls = jnp.concatenate([s[2] for s in streams], axis=0)
    nt, _, tm = cls.shape
    n_sorted_tiles = nt * tm // SORT_TILE + N_CLASSES
    pos, tcls = _rank_call(cls, n_sorted_tiles)
    pos = pos.reshape(nt, tm)
    buf = jnp.zeros((n_sorted_tiles * SORT_TILE, streams[0][1].shape[1]), F32)
    bounds = [0]
    for s in streams:
        bounds.append(bounds[-1] + s[2].shape[0])
        buf = _dispatch_call(pos[bounds[-2]:bounds[-1]], s[1], buf)
    ys = _ffn_call(tcls[0, :n_sorted_tiles], _class_experts(), buf, wg, wu, wd)
    return [_combine_call(pos[bounds[k]:bounds[k + 1]], s[0], s[3], ys) for k, s in enumerate(streams)]


def _rope_tables(n_tokens):
    rows = n_tokens // GRID_W
    row_ids = jnp.repeat(jnp.arange(rows), GRID_W).astype(F32)
    col_ids = jnp.tile(jnp.arange(GRID_W), rows).astype(F32)
    half = HEAD_DIM // 2
    inv = ROPE_THETA ** (-jnp.arange(0, half, 2, dtype=F32) / half)
    ang_r = row_ids[:, None] * inv
    ang_c = col_ids[:, None] * inv
    ang = jnp.concatenate([ang_r, ang_r, ang_c, ang_c] * (LANES // HEAD_DIM), axis=-1)
    return jnp.cos(ang), jnp.sin(ang)


def _head_mean_matrix():
    i = lax.broadcasted_iota(jnp.int32, (LANES, LANES), 0) // HEAD_DIM
    j = lax.broadcasted_iota(jnp.int32, (LANES, LANES), 1) // HEAD_DIM
    return jnp.where(i == j, 1.0 / HEAD_DIM, 0.0).astype(BF16)


def kernel(x, c, ctx, c_ctx, w_mod, b_mod, g_norm1, g_norm2, w_in, q_norm, k_norm, hy_short_w, hy_short_b, hy_w1, hy_b1, hy_w2, hy_b2, hy_w3, hy_freq, hy_bias, cf_dw_w, cf_dw_b, cf_ln_g, cf_ln_b, p_attn, p_hyena, p_conv, w_out, w_router, b_router, w_exp_gate, w_exp_up, w_exp_down):
    batch, seq, d = x.shape
    n_ctx = ctx.shape[1]
    depth = w_mod.shape[0]
    hy_w = hy_short_w.shape[2]
    cf_w = 2 * cf_dw_w.shape[2]
    tm = 512

    pad_rows = (-(batch + 1)) % 8
    cc = jnp.concatenate([c, c_ctx[None, :], jnp.zeros((pad_rows, d), F32)], axis=0)
    mods = _mod_call(cc, w_mod, b_mod)

    cos_x, sin_x = _rope_tables(seq)
    cos_c = jnp.ones((tm, LANES), F32)
    sin_c = jnp.zeros((tm, LANES), F32)
    hm = _head_mean_matrix()
    tile2 = lambda g: jnp.tile(g, (1, LANES // HEAD_DIM))
    qg, kg = tile2(q_norm), tile2(k_norm)

    w1p = jnp.pad(hy_w1, ((0, 0), (0, LANES - hy_w1.shape[1]), (0, 0)))
    r3 = lambda a: a[:, None, :]
    tabs_x = _dft_tables(seq)
    tabs_c = _dft_tables(n_ctx)
    filt_args = (w1p, r3(hy_b1), hy_w2, r3(hy_b2), hy_w3, r3(hy_freq))
    kr_x, ki_x, kn_x = _hyena_filter_call(seq, tabs_x, *filt_args)
    kr_c, ki_c, kn_c = _hyena_filter_call(n_ctx, tabs_c, *filt_args)

    bf = lambda a: a.astype(BF16)
    w_in_b, pa_b, ph_b, pc_b, wo_b = bf(w_in), bf(p_attn), bf(p_hyena), bf(p_conv), bf(w_out)
    wg_b, wu_b, wd_b = bf(w_exp_gate), bf(w_exp_up), bf(w_exp_down)
    wr_t = w_router.T
    br = b_router[:, None]

    xl = x.reshape(batch * seq, d)
    xc = ctx.reshape(batch * n_ctx, d)
    for l in range(depth):
        last = l == depth - 1
        mod_x = mods[l, :batch][:, None, :]
        mod_c = mods[l, batch:batch + 1][:, None, :]
        row = lambda a: a[l][None, :]
        streams = [(xl, mod_x, seq, cos_x, sin_x, tabs_x, kr_x, ki_x, kn_x),
                   (xc, mod_c, n_ctx, cos_c, sin_c, tabs_c, kr_c, ki_c, kn_c)]
        proj = [
            _inproj_call(s[0], s[1], row(g_norm1), w_in_b[l], s[3], s[4], qg[l:l + 1], kg[l:l + 1], hm,
                         tm=tm, hy_w=hy_w, cf_w=cf_w)
            for s in streams]
        (q_x, k_x, v_x, zhy_x, zcf_x, zg_x), (q_c, k_c, v_c, zhy_c, zcf_c, zg_c) = proj
        routed = []
        for si, s in enumerate(streams):
            if si == 1 and last:
                break
            xs, mod, n, _, _, tabs, kr, ki, kn = s
            if si == 0:
                oa = _attn_call(q_x, [(k_x, v_x), (k_c, v_c)], batch=batch, tq=256)
                zhy, zcf, zg = zhy_x, zcf_x, zg_x
            else:
                oa = _attn_call(q_c, [(k_c, v_c)], batch=batch, tq=n_ctx)
                zhy, zcf, zg = zhy_c, zcf_c, zg_c
            oh = _hyena_call(zhy, hy_short_w[l], row(hy_short_b), tabs[0], tabs[2], kr[l], ki[l], kn[l],
                             row(hy_bias), n=n)
            oc = _conf_call(zcf, cf_dw_w[l], row(cf_dw_b), row(cf_ln_g), row(cf_ln_b), n=n)
            x1, hx, cls = _merge_call(oa, oh, oc, zg, xs, mod, row(g_norm2), pa_b[l], ph_b[l], pc_b[l],
                                      wo_b[l], wr_t, br, tm=tm)
            routed.append((x1, hx, cls, mod))
        new = _moe(routed, wg_b[l], wu_b[l], wd_b[l])
        xl = new[0]
        if not last:
            xc = new[1]
    return xl.reshape(batch, seq, d)
```

```python
import functools
import math

import jax
import jax.numpy as jnp
from jax import lax
from jax.experimental import pallas as pl
from jax.experimental.pallas import tpu as pltpu

F32 = jnp.float32
BF16 = jnp.bfloat16

NORM_EPS = 1e-6
N_MOD = 6
HEAD_DIM = 64
N_Q_HEADS = 8
N_KV_HEADS = 2
ATTN_W = N_Q_HEADS * HEAD_DIM
KV_W = N_KV_HEADS * HEAD_DIM
GRID_W = 64
ROPE_THETA = 10000.0
HYENA_BANDS = 16
HYENA_FAST_DECAY = 0.3
HYENA_SLOW_DECAY = 1.5
HYENA_TARGET = 1e-2
CONF_K = 31
N_EXPERTS = 16
N_GROUPS = 4
GROUP_SIZE = N_EXPERTS // N_GROUPS

Q_SCALE = math.log2(math.e) / math.sqrt(HEAD_DIM)

LANES = 128
SUBLANES = 8
VMEM_LIMIT = 56 * 1024 * 1024


def _cparams(*sem):
    return pltpu.CompilerParams(dimension_semantics=sem, vmem_limit_bytes=VMEM_LIMIT)


def _dot(a, b):
    return jnp.dot(a, b, preferred_element_type=F32)


def _split(a):
    hi = a.astype(BF16)
    lo = (a - hi.astype(F32)).astype(BF16)
    return hi, lo


def _dot3(a, b):
    ah, al = _split(a)
    bh, bl = _split(b)
    return _dot(ah, bh) + (_dot(al, bh) + _dot(ah, bl))


def _silu(x):
    return x * jax.nn.sigmoid(x)


def _mod_kernel(c_ref, w_ref, b_ref, o_ref):
    o_ref[0] = _dot3(_silu(c_ref[...]), w_ref[0]) + b_ref[0]


def _mod_call(cc, w_mod, b_mod):
    depth, d, n = w_mod.shape
    r = cc.shape[0]
    tn = 1536
    return pl.pallas_call(
        _mod_kernel,
        grid=(depth, n // tn),
        in_specs=[pl.BlockSpec((r, d), lambda l, j: (0, 0)),
                  pl.BlockSpec((1, d, tn), lambda l, j: (l, 0, j)),
                  pl.BlockSpec((1, 1, tn), lambda l, j: (l, 0, j))],
        out_specs=pl.BlockSpec((1, r, tn), lambda l, j: (l, 0, j)),
        out_shape=jax.ShapeDtypeStruct((depth, r, n), F32),
        compiler_params=_cparams("arbitrary", "arbitrary"),
        name="adaln_mod",
    )(cc, w_mod, b_mod.reshape(depth, 1, n))


def _inproj_kernel(x_ref, mod_ref, g_ref, w_ref, cos_ref, sin_ref, qg_ref, kg_ref, hm_ref,
                   q_ref, k_ref, v_ref, zhy_ref, zcf_ref, zg_ref):
    d = x_ref.shape[1]
    x = x_ref[...]
    ms = jnp.mean(x * x, axis=-1, keepdims=True)
    y = x * lax.rsqrt(ms + NORM_EPS) * g_ref[...]
    sh = mod_ref[0, :, 0:d]
    sc = mod_ref[0, :, d:2 * d]
    h = (y * (1.0 + sc) + sh).astype(BF16)

    cos = cos_ref[...]
    sin = sin_ref[...]
    hm = hm_ref[...]
    lane = lax.broadcasted_iota(jnp.int32, (1, LANES), 1)
    first_quarter = (lane % 32) < 16

    def norm_rope(z, gain, scale):
        hi, lo = _split(z * z)
        msq = _dot(hi, hm) + _dot(lo, hm)
        zn = z * lax.rsqrt(msq + NORM_EPS) * gain
        rot = jnp.where(first_quarter, -pltpu.roll(zn, LANES - 16, 1), pltpu.roll(zn, 16, 1))
        return (zn * cos + rot * sin) * scale

    zq = _dot(h, w_ref[:, 0:ATTN_W])
    for cb in range(ATTN_W // LANES):
        sl = slice(cb * LANES, (cb + 1) * LANES)
        q_ref[:, sl] = norm_rope(zq[:, sl], qg_ref[...], Q_SCALE).astype(BF16)
    zkv = _dot(h, w_ref[:, ATTN_W:ATTN_W + 2 * KV_W])
    k_ref[...] = norm_rope(zkv[:, 0:KV_W], kg_ref[...], 1.0).astype(BF16)
    v_ref[...] = zkv[:, KV_W:2 * KV_W].astype(BF16)
    off = ATTN_W + 2 * KV_W
    for ref in (zhy_ref, zcf_ref):
        w = ref.shape[1]
        ref[...] = _dot(h, w_ref[:, off:off + w])
        off += w
    chunk = 1024
    for cb in range(zg_ref.shape[1] // chunk):
        zg_ref[:, cb * chunk:(cb + 1) * chunk] = _dot(
            h, w_ref[:, off + cb * chunk:off + (cb + 1) * chunk])


def _inproj_call(x, mod, g1, w_in, cos, sin, qg, kg, hm, *, tm, hy_w, cf_w):
    t, d = x.shape
    tiles_per_mod = t // mod.shape[0] // tm
    n = w_in.shape[1]
    g_w = n - ATTN_W - 2 * KV_W - hy_w - cf_w
    n_pos = cos.shape[0] // tm
    row = lambda i: (i, 0)
    const = lambda i: (0, 0)
    return pl.pallas_call(
        _inproj_kernel,
        grid=(t // tm,),
        in_specs=[pl.BlockSpec((tm, d), row),
                  pl.BlockSpec((1, 1, mod.shape[2]), lambda i: (i // tiles_per_mod, 0, 0)),
                  pl.BlockSpec((1, d), const),
                  pl.BlockSpec((d, n), const),
                  pl.BlockSpec((tm, LANES), lambda i: (i % n_pos, 0)),
                  pl.BlockSpec((tm, LANES), lambda i: (i % n_pos, 0)),
                  pl.BlockSpec((1, LANES), const),
                  pl.BlockSpec((1, LANES), const),
                  pl.BlockSpec((LANES, LANES), const)],
        out_specs=[pl.BlockSpec((tm, ATTN_W), row),
                   pl.BlockSpec((tm, KV_W), row),
                   pl.BlockSpec((tm, KV_W), row),
                   pl.BlockSpec((tm, hy_w), row),
                   pl.BlockSpec((tm, cf_w), row),
                   pl.BlockSpec((tm, g_w), row)],
        out_shape=[jax.ShapeDtypeStruct((t, ATTN_W), BF16),
                   jax.ShapeDtypeStruct((t, KV_W), BF16),
                   jax.ShapeDtypeStruct((t, KV_W), BF16),
                   jax.ShapeDtypeStruct((t, hy_w), F32),
                   jax.ShapeDtypeStruct((t, cf_w), F32),
                   jax.ShapeDtypeStruct((t, g_w), F32)],
        compiler_params=_cparams("arbitrary"),
        name="in_proj",
    )(x, mod, g1, w_in, cos, sin, qg, kg, hm)


def _attn_kernel(*refs, n_src):
    q_ref = refs[0]
    kv_refs = refs[1:1 + 2 * n_src]
    o_ref = refs[1 + 2 * n_src]
    kk_ref, vv_ref = refs[2 + 2 * n_src:]
    h = pl.program_id(1)
    lane = lax.broadcasted_iota(jnp.int32, (1, LANES), 1)
    lo_half = lane < HEAD_DIM

    @pl.when(pl.program_id(2) == 0)
    def _():
        keep = (lane // HEAD_DIM) == h
        off = 0
        for s in range(n_src):
            n = kv_refs[2 * s].shape[0]
            k = kv_refs[2 * s][...].astype(F32)
            kk_ref[off:off + n, :] = jnp.where(keep, k, pltpu.roll(k, HEAD_DIM, 1)).astype(BF16)
            v = kv_refs[2 * s + 1][...].astype(F32)
            v = jnp.where(keep, v, pltpu.roll(v, HEAD_DIM, 1))
            vv_ref[off:off + n, :] = jnp.where(lo_half, v, 1.0).astype(BF16)
            off += n

    kk = kk_ref[...]
    vv = vv_ref[...]
    n_heads = 2 * (q_ref.shape[1] // LANES)

    def scores(hd):
        q2 = q_ref[:, (hd // 2) * LANES:(hd // 2 + 1) * LANES]
        msk = lo_half if hd % 2 == 0 else jnp.logical_not(lo_half)
        qm = jnp.where(msk, q2, jnp.zeros_like(q2))
        return lax.dot_general(qm, kk, (((1,), (1,)), ((), ())), preferred_element_type=F32)

    def probs(s):
        return jnp.exp2(s - jnp.max(s, axis=-1, keepdims=True)).astype(BF16)

    def values(p):
        o = _dot(p, vv)
        return o / o[:, HEAD_DIM:HEAD_DIM + 1]

    s_all = {0: scores(0)}
    outs = {}
    p_prev = None
    for hd in range(n_heads):
        if hd + 1 < n_heads:
            s_all[hd + 1] = scores(hd + 1)
        p_cur = probs(s_all.pop(hd))
        if p_prev is not None:
            outs[hd - 1] = values(p_prev)
        p_prev = p_cur
    outs[n_heads - 1] = values(p_prev)
    for j in range(n_heads // 2):
        pair = jnp.where(lo_half, outs[2 * j], pltpu.roll(outs[2 * j + 1], HEAD_DIM, 1))
        o_ref[:, j * LANES:(j + 1) * LANES] = pair.astype(BF16)


def _attn_call(q, kvs, *, batch, tq):
    t = q.shape[0]
    nq = t // batch // tq
    gw = ATTN_W // N_KV_HEADS
    in_specs = [pl.BlockSpec((tq, gw), lambda b, h, i: (b * nq + i, h))]
    args = [q]
    nk = 0
    for k, v in kvs:
        n = k.shape[0] // batch
        nk += n
        for a in (k, v):
            in_specs.append(pl.BlockSpec((n, KV_W), lambda b, h, i: (b, 0)))
            args.append(a)
    return pl.pallas_call(
        functools.partial(_attn_kernel, n_src=len(kvs)),
        grid=(batch, N_KV_HEADS, nq),
        in_specs=in_specs,
        out_specs=pl.BlockSpec((tq, gw), lambda b, h, i: (b * nq + i, h)),
        out_shape=jax.ShapeDtypeStruct((t, ATTN_W), BF16),
        scratch_shapes=[pltpu.VMEM((nk, LANES), BF16), pltpu.VMEM((nk, LANES), BF16)],
        compiler_params=_cparams("arbitrary", "arbitrary", "arbitrary"),
        name="attention",
    )(*args)


def _hy_time_kernel(feat_ref, w1_ref, b1_ref, w2_ref, b2_ref, w3_ref, fq_ref, dec_ref,
                    sp_ref, sm_ref, kn_ref):
    n = feat_ref.shape[0]
    fq = fq_ref[0]
    h = jnp.sin(fq * (_dot3(feat_ref[...], w1_ref[0]) + b1_ref[0]))
    h = jnp.sin(fq * (_dot3(h, w2_ref[0]) + b2_ref[0]))
    h = _dot3(h, w3_ref[0]) * dec_ref[...]
    h = h / jnp.sum(jnp.abs(h), axis=0, keepdims=True)
    c = h.shape[1] // 2
    row = lax.broadcasted_iota(jnp.int32, (n, 1), 0)
    hf = h[:, 0:c]
    hb = jnp.where(row == 0, 0.0, h[:, c:2 * c])
    sp = hf + hb
    sp_ref[0] = sp
    sm_ref[0] = hf - hb
    sgn = (1 - 2 * (row & 1)).astype(F32)
    kn_ref[0] = jnp.sum(sp * sgn, axis=0, keepdims=True) * (0.5 / n)


def _hy_spec_kernel(fch_ref, fcl_ref, fsh_ref, fsl_ref, sp_ref, sm_ref, kr_ref, ki_ref, *, n):
    tk = fch_ref.shape[0]
    k = pl.program_id(1) * tk + lax.broadcasted_iota(jnp.int32, (tk, 1), 0)
    scl = jnp.where(k == 0, 0.5 / n, 1.0 / n)
    sph, spl = _split(sp_ref[0])
    smh, sml = _split(sm_ref[0])
    kr = _dot(fch_ref[...], sph) + (_dot(fcl_ref[...], sph) + _dot(fch_ref[...], spl))
    ki = _dot(fsh_ref[...], smh) + (_dot(fsl_ref[...], smh) + _dot(fsh_ref[...], sml))
    kr_ref[0] = kr * scl
    ki_ref[0] = ki * scl


def _dft_tables(n):
    k = lax.broadcasted_iota(jnp.int32, (n, n), 0)
    t = lax.broadcasted_iota(jnp.int32, (n, n), 1)
    ang = ((k * t) % (2 * n)).astype(F32) * (math.pi / n)
    return _split(jnp.cos(ang)) + _split(-jnp.sin(ang))


def _hyena_features(n, c):
    t = jnp.linspace(0.0, 1.0, n, dtype=F32)[:, None]
    w = 2.0 * math.pi * jnp.arange(n, dtype=F32)[:, None] / n
    f = jnp.linspace(1e-4, HYENA_BANDS - 1, HYENA_BANDS, dtype=F32)[None, :]
    feat = jnp.concatenate([t, jnp.cos(f * w), -jnp.sin(f * w)], axis=-1)
    feat = jnp.pad(feat, ((0, 0), (0, LANES - feat.shape[1])))
    max_decay = math.log(HYENA_TARGET) / HYENA_FAST_DECAY
    min_decay = math.log(HYENA_TARGET) / HYENA_SLOW_DECAY
    deltas = jnp.tile(jnp.linspace(min_decay, max_decay, c, dtype=F32), 2)
    return feat, jnp.exp(-t * jnp.abs(deltas))


def _hyena_filter_call(n, tabs, w1p, b1, w2, b2, w3, fq):
    depth = w1p.shape[0]
    ffn = w2.shape[1]
    c = w3.shape[2] // 2
    feat, dec = _hyena_features(n, c)
    lay = lambda l: (l, 0, 0)
    const = lambda l: (0, 0)
    sp, sm, kn = pl.pallas_call(
        _hy_time_kernel,
        grid=(depth,),
        in_specs=[pl.BlockSpec((n, LANES), const),
                  pl.BlockSpec((1, LANES, ffn), lay), pl.BlockSpec((1, 1, ffn), lay),
                  pl.BlockSpec((1, ffn, ffn), lay), pl.BlockSpec((1, 1, ffn), lay),
                  pl.BlockSpec((1, ffn, 2 * c), lay), pl.BlockSpec((1, 1, ffn), lay),
                  pl.BlockSpec((n, 2 * c), const)],
        out_specs=[pl.BlockSpec((1, n, c), lay), pl.BlockSpec((1, n, c), lay),
                   pl.BlockSpec((1, 1, c), lay)],
        out_shape=[jax.ShapeDtypeStruct((depth, n, c), F32), jax.ShapeDtypeStruct((depth, n, c), F32),
                   jax.ShapeDtypeStruct((depth, 1, c), F32)],
        compiler_params=_cparams("arbitrary"),
        name="hyena_filter_time",
    )(feat, w1p, b1, w2, b2, w3, fq, dec)
    tk = min(n, 512)
    mat = pl.BlockSpec((tk, n), lambda l, j: (j, 0))
    vec = pl.BlockSpec((1, n, c), lambda l, j: (l, 0, 0))
    out = pl.BlockSpec((1, tk, c), lambda l, j: (l, j, 0))
    kr, ki = pl.pallas_call(
        functools.partial(_hy_spec_kernel, n=n),
        grid=(depth, n // tk),
        in_specs=[mat, mat, mat, mat, vec, vec],
        out_specs=[out, out],
        out_shape=[jax.ShapeDtypeStruct((depth, n, c), F32)] * 2,
        compiler_params=_cparams("arbitrary", "arbitrary"),
        name="hyena_filter_spectrum",
    )(*tabs, sp, sm)
    return kr, ki, kn


HYENA_ROW_CHUNK = 512


def _hyena_kernel(z_ref, sw_ref, sb_ref, fc_ref, fs_ref, kr_ref, ki_ref, kn_ref, db_ref, o_ref):
    n = z_ref.shape[0]
    c = o_ref.shape[1]
    z = z_ref[...]
    row = lax.broadcasted_iota(jnp.int32, (n, 1), 0)
    z_prev = jnp.where(row == 0, 0.0, pltpu.roll(z, 1, 0))
    z_next = jnp.where(row == n - 1, 0.0, pltpu.roll(z, n - 1, 0))
    zc = z_prev * sw_ref[0:1, :] + z * sw_ref[1:2, :] + z_next * sw_ref[2:3, :] + sb_ref[...]
    x0 = zc[:, 0:c]
    u = zc[:, 2 * c:3 * c] * zc[:, c:2 * c]
    ub = u.astype(BF16)
    sgn = (1 - 2 * (row & 1)).astype(F32)
    nyq = jnp.sum(u * sgn, axis=0, keepdims=True) * kn_ref[...]
    chunk = min(n, HYENA_ROW_CHUNK)
    chunks = [slice(r, r + chunk) for r in range(0, n, chunk)]
    pr, pi = [], []
    for rs in chunks:
        ur = _dot(fc_ref[rs, :], ub)
        ui = _dot(fs_ref[rs, :], ub)
        kr = kr_ref[rs, :]
        ki = ki_ref[rs, :]
        pr.append((ur * kr - ui * ki).astype(BF16))
        pi.append((ur * ki + ui * kr).astype(BF16))
    pr = jnp.concatenate(pr, axis=0)
    pi = jnp.concatenate(pi, axis=0)
    for rs in chunks:
        y = _dot(fc_ref[rs, :], pr) + _dot(fs_ref[rs, :], pi) + sgn[rs, :] * nyq
        o_ref[rs, :] = ((y + u[rs, :] * db_ref[...]) * x0[rs, :]).astype(BF16)


def _hyena_call(z, sw, sb, fc, fs, kr, ki, kn, db, *, n):
    t, w = z.shape
    c = w // 3
    const = lambda b: (0, 0)
    return pl.pallas_call(
        _hyena_kernel,
        grid=(t // n,),
        in_specs=[pl.BlockSpec((n, w), lambda b: (b, 0)),
                  pl.BlockSpec((3, w), const), pl.BlockSpec((1, w), const),
                  pl.BlockSpec((n, n), const), pl.BlockSpec((n, n), const),
                  pl.BlockSpec((n, c), const), pl.BlockSpec((n, c), const),
                  pl.BlockSpec((1, c), const), pl.BlockSpec((1, c), const)],
        out_specs=pl.BlockSpec((n, c), lambda b: (b, 0)),
        out_shape=jax.ShapeDtypeStruct((t, c), BF16),
        compiler_params=_cparams("arbitrary"),
        name="hyena_conv",
    )(z, sw, sb, fc, fs, kr, ki, kn, db)


CONF_PAD = 16


def _conf_kernel(z_ref, w_ref, b_ref, g_ref, beta_ref, o_ref, pad_ref):
    n = z_ref.shape[0]
    c = o_ref.shape[1]
    u = z_ref[:, 0:c] * jax.nn.sigmoid(z_ref[:, c:2 * c])
    zeros = jnp.zeros((CONF_PAD, c), F32)
    pad_ref[0:CONF_PAD, :] = zeros
    pad_ref[CONF_PAD + n:2 * CONF_PAD + n, :] = zeros
    pad_ref[CONF_PAD:CONF_PAD + n, :] = u
    first = CONF_PAD - (CONF_K - 1) // 2
    acc = b_ref[...]
    for r in range(SUBLANES):
        part = None
        for off in range(r, first + CONF_K, SUBLANES):
            j = off - first
            if j < 0:
                continue
            term = w_ref[j:j + 1, :] * pad_ref[off - r:off - r + n + SUBLANES, :]
            part = term if part is None else part + term
        acc = acc + part[r:r + n, :]
    mu = jnp.mean(acc, axis=-1, keepdims=True)
    dlt = acc - mu
    var = jnp.mean(dlt * dlt, axis=-1, keepdims=True)
    y = dlt * lax.rsqrt(var + NORM_EPS) * g_ref[...] + beta_ref[...]
    o_ref[...] = _silu(y).astype(BF16)


def _conf_call(z, w, b, g, beta, *, n):
    t, w2 = z.shape
    c = w2 // 2
    const = lambda i: (0, 0)
    return pl.pallas_call(
        _conf_kernel,
        grid=(t // n,),
        in_specs=[pl.BlockSpec((n, w2), lambda i: (i, 0)),
                  pl.BlockSpec((CONF_K, c), const), pl.BlockSpec((1, c), const),
                  pl.BlockSpec((1, c), const), pl.BlockSpec((1, c), const)],
        out_specs=pl.BlockSpec((n, c), lambda i: (i, 0)),
        out_shape=jax.ShapeDtypeStruct((t, c), BF16),
        scratch_shapes=[pltpu.VMEM((n + 2 * CONF_PAD, c), F32)],
        compiler_params=_cparams("arbitrary"),
        name="conformer_conv",
    )(z, w, b, g, beta)


def _route(sel, s):
    grp_score = []
    for g in range(N_GROUPS):
        r = sel[g * GROUP_SIZE:(g + 1) * GROUP_SIZE]
        best = None
        for i in range(GROUP_SIZE):
            for j in range(i + 1, GROUP_SIZE):
                pair = r[i] + r[j]
                best = pair if best is None else jnp.maximum(best, pair)
        grp_score.append(best)
    gbest = grp_score[0]
    gidx = jnp.zeros_like(gbest, dtype=jnp.int32)
    for g in range(1, N_GROUPS):
        better = grp_score[g] > gbest
        gidx = jnp.where(better, g, gidx)
        gbest = jnp.where(better, grp_score[g], gbest)
    neg = jnp.full_like(gbest, -jnp.inf)
    masked = [jnp.where(gidx == e // GROUP_SIZE, sel[e], neg) for e in range(N_EXPERTS)]

    def argmax_first(vals):
        best = vals[0]
        idx = jnp.zeros_like(gidx)
        for e in range(1, N_EXPERTS):
            better = vals[e] > best
            idx = jnp.where(better, e, idx)
            best = jnp.where(better, vals[e], best)
        return idx

    i1 = argmax_first(masked)
    i2 = argmax_first([jnp.where(i1 == e, neg, masked[e]) for e in range(N_EXPERTS)])
    zero = jnp.zeros_like(gbest)
    w1 = sum(jnp.where(i1 == e, s[e], zero) for e in range(N_EXPERTS))
    w2 = sum(jnp.where(i2 == e, s[e], zero) for e in range(N_EXPERTS))
    tot = w1 + w2
    w1, w2 = w1 / tot, w2 / tot
    swap = i2 < i1
    a = jnp.where(swap, i2, i1) - GROUP_SIZE * gidx
    b = jnp.where(swap, i1, i2) - GROUP_SIZE * gidx
    cls = PAIRS_PER_GROUP * gidx + ((a * (2 * GROUP_SIZE - 1 - a)) >> 1) + (b - a - 1)
    return cls, jnp.where(swap, w2, w1), jnp.where(swap, w1, w2)


MERGE_ROW_CHUNK = 512


def _merge_kernel(oa_ref, oh_ref, oc_ref, zg_ref, x_ref, mod_ref, g2_ref,
                  pa_ref, ph_ref, pc_ref, wo_ref, wr_ref, br_ref,
                  xo_ref, hx_ref, cls_ref):
    d = x_ref.shape[1]
    wh, wl = _split(wr_ref[...])
    for r0 in range(0, x_ref.shape[0], MERGE_ROW_CHUNK):
        rs = slice(r0, r0 + MERGE_ROW_CHUNK)
        merged = jax.nn.sigmoid(zg_ref[rs, 0:d]) * _dot(oa_ref[rs, :], pa_ref[...])
        merged += jax.nn.sigmoid(zg_ref[rs, d:2 * d]) * _dot(oh_ref[rs, :], ph_ref[...])
        merged += jax.nn.sigmoid(zg_ref[rs, 2 * d:3 * d]) * _dot(oc_ref[rs, :], pc_ref[...])
        y = _dot(merged.astype(BF16), wo_ref[...])
        x = x_ref[rs, :] + mod_ref[0, :, 2 * d:3 * d] * y
        xo_ref[rs, :] = x
        ms = jnp.mean(x * x, axis=-1, keepdims=True)
        h2 = x * lax.rsqrt(ms + NORM_EPS) * g2_ref[...]
        h2 = h2 * (1.0 + mod_ref[0, :, 4 * d:5 * d]) + mod_ref[0, :, 3 * d:4 * d]
        hx_ref[rs, 0:d] = h2
        hh, hl = _split(h2)
        nt = (((1,), (1,)), ((), ()))
        logits = (lax.dot_general(wh, hh, nt, preferred_element_type=F32)
                  + (lax.dot_general(wh, hl, nt, preferred_element_type=F32)
                     + lax.dot_general(wl, hh, nt, preferred_element_type=F32)))
        s = jax.nn.sigmoid(logits)
        sel = s + br_ref[...]
        cls, w_lo, w_hi = _route([sel[e:e + 1, :] for e in range(N_EXPERTS)],
                                 [s[e:e + 1, :] for e in range(N_EXPERTS)])
        cls_ref[0, :, rs] = cls
        sub = lax.broadcasted_iota(jnp.int32, (LANES, MERGE_ROW_CHUNK), 0)
        rows = jnp.where(sub == 0, w_lo, jnp.where(sub == 1, w_hi, 0.0))
        hx_ref[rs, d:d + LANES] = rows.T


def _merge_call(oa, oh, oc, zg, x, mod, g2, pa, ph, pc, wo, wr_t, br, *, tm):
    t, d = x.shape
    tiles_per_mod = t // mod.shape[0] // tm
    row = lambda i: (i, 0)
    const = lambda i: (0, 0)
    full = lambda a: pl.BlockSpec(a.shape, const)
    return pl.pallas_call(
        _merge_kernel,
        grid=(t // tm,),
        in_specs=[pl.BlockSpec((tm, oa.shape[1]), row), pl.BlockSpec((tm, oh.shape[1]), row),
                  pl.BlockSpec((tm, oc.shape[1]), row), pl.BlockSpec((tm, zg.shape[1]), row),
                  pl.BlockSpec((tm, d), row),
                  pl.BlockSpec((1, 1, mod.shape[2]), lambda i: (i // tiles_per_mod, 0, 0)),
                  full(g2), full(pa), full(ph), full(pc), full(wo), full(wr_t), full(br)],
        out_specs=[pl.BlockSpec((tm, d), row), pl.BlockSpec((tm, d + LANES), row),
                   pl.BlockSpec((1, 1, tm), lambda i: (i, 0, 0))],
        out_shape=[jax.ShapeDtypeStruct((t, d), F32), jax.ShapeDtypeStruct((t, d + LANES), F32),
                   jax.ShapeDtypeStruct((t // tm, 1, tm), jnp.int32)],
        compiler_params=_cparams("arbitrary"),
        name="merge_router",
    )(oa, oh, oc, zg, x, mod, g2, pa, ph, pc, wo, wr_t, br)


PAIRS_PER_GROUP = GROUP_SIZE * (GROUP_SIZE - 1) // 2
N_CLASSES = N_GROUPS * PAIRS_PER_GROUP
SORT_TILE = 512


def _class_experts():
    lo, hi = [], []
    for g in range(N_GROUPS):
        for a in range(GROUP_SIZE):
            for b in range(a + 1, GROUP_SIZE):
                lo.append(g * GROUP_SIZE + a)
                hi.append(g * GROUP_SIZE + b)
    return jnp.asarray(lo + hi, dtype=jnp.int32)


def _col(row):
    return jnp.broadcast_to(row, (LANES, row.shape[1])).T[:, 0:1]


def _row(col):
    return jnp.broadcast_to(col, (col.shape[0], LANES)).T[0:1, :]


def _rank_kernel(cls_ref, pos_ref, tcls_ref, cnt_ref, base_ref):
    phase = pl.program_id(0)
    i = pl.program_id(1)
    tm = cls_ref.shape[2]
    cls_col = _col(cls_ref[0].astype(F32)).astype(jnp.int32)
    lane = lax.broadcasted_iota(jnp.int32, (1, LANES), 1)
    onehot = (cls_col == lane).astype(F32)
    counts = jnp.sum(onehot, axis=0, keepdims=True)

    @pl.when(phase == 0)
    def _():
        @pl.when(i == 0)
        def _():
            cnt_ref[...] = jnp.zeros_like(cnt_ref)
        cnt_ref[...] += counts

    @pl.when(phase == 1)
    def _():
        @pl.when(i == 0)
        def _():
            shift = SORT_TILE.bit_length() - 1
            ntile = lax.shift_right_logical(cnt_ref[...].astype(jnp.int32) + (SORT_TILE - 1), shift)
            ntile = ntile.astype(F32)
            r = lax.broadcasted_iota(jnp.int32, (LANES, LANES), 0)
            c = lax.broadcasted_iota(jnp.int32, (LANES, LANES), 1)
            start = _dot3(jnp.broadcast_to(ntile, (8, LANES)), (r < c).astype(F32))[0:1, :]
            base_ref[...] = start * SORT_TILE
            end_col = _col(start + ntile)
            tile = lax.broadcasted_iota(jnp.int32, (1, tcls_ref.shape[1]), 1).astype(F32)
            tcls_ref[...] = jnp.sum((end_col <= tile).astype(F32), axis=0,
                                    keepdims=True).astype(jnp.int32)
        t = lax.broadcasted_iota(jnp.int32, (tm, tm), 0)
        s = lax.broadcasted_iota(jnp.int32, (tm, tm), 1)
        earlier = _dot((s < t).astype(BF16), onehot.astype(BF16))
        pos_col = jnp.sum(onehot * (earlier + base_ref[...]), axis=1, keepdims=True)
        pos_ref[0] = _row(pos_col).astype(jnp.int32)
        base_ref[...] += counts


def _rank_call(cls, n_sorted_tiles):
    nt, _, tm = cls.shape
    nsp = -(-n_sorted_tiles // LANES) * LANES
    return pl.pallas_call(
        _rank_kernel,
        grid=(2, nt),
        in_specs=[pl.BlockSpec((1, 1, tm), lambda p, i: (i, 0, 0))],
        out_specs=[pl.BlockSpec((1, 1, tm), lambda p, i: (i * p, 0, 0)),
                   pl.BlockSpec((1, nsp), lambda p, i: (0, 0))],
        out_shape=[jax.ShapeDtypeStruct((nt, 1, tm), jnp.int32),
                   jax.ShapeDtypeStruct((1, nsp), jnp.int32)],
        scratch_shapes=[pltpu.VMEM((1, LANES), F32), pltpu.VMEM((1, LANES), F32)],
        compiler_params=_cparams("arbitrary", "arbitrary"),
        name="moe_rank",
    )(cls)


def _row_copies(n, copy_of_row, op):
    def group(g, carry):
        base = pl.multiple_of(g * SUBLANES, SUBLANES)
        for j in range(SUBLANES):
            cp = copy_of_row(base, j)
            if op == "start":
                cp.start(priority=j % 2)
            else:
                cp.wait()
        return carry

    lax.fori_loop(0, n // SUBLANES, group, 0)


def _dispatch_kernel(pos_ref, hx_ref, buf_in, buf_out, sem):
    del buf_in
    tm = hx_ref.shape[0]
    first = pl.program_id(0) * tm

    def copy_of_row(base, j):
        return pltpu.make_async_copy(hx_ref.at[pl.ds(base + j, 1)],
                                     buf_out.at[pl.ds(pos_ref[first + base + j], 1)], sem)

    _row_copies(tm, copy_of_row, "start")
    _row_copies(tm, copy_of_row, "wait")


def _dispatch_call(pos, hx, buf, *, tm):
    t, w = hx.shape
    any_spec = pl.BlockSpec(memory_space=pl.ANY)
    return pl.pallas_call(
        _dispatch_kernel,
        grid_spec=pltpu.PrefetchScalarGridSpec(
            num_scalar_prefetch=1,
            grid=(t // tm,),
            in_specs=[pl.BlockSpec((tm, w), lambda i, *_: (i, 0)), any_spec],
            out_specs=any_spec,
            scratch_shapes=[pltpu.SemaphoreType.DMA]),
        out_shape=jax.ShapeDtypeStruct(buf.shape, buf.dtype),
        input_output_aliases={2: 0},
        compiler_params=_cparams("arbitrary"),
        name="moe_dispatch",
    )(pos, hx, buf)


def _ffn_kernel(tcls_ref, etab_ref, xs_ref, wga_ref, wua_ref, wda_ref, wgb_ref, wub_ref, wdb_ref, o_ref):
    del etab_ref
    d = o_ref.shape[1]
    valid = tcls_ref[pl.program_id(0)] < N_CLASSES

    @pl.when(valid)
    def _():
        x = xs_ref[:, 0:d].astype(BF16)
        ha = _silu(_dot(x, wga_ref[0])) * _dot(x, wua_ref[0]) * xs_ref[:, d:d + 1]
        hb = _silu(_dot(x, wgb_ref[0])) * _dot(x, wub_ref[0]) * xs_ref[:, d + 1:d + 2]
        o_ref[...] = _dot(ha.astype(BF16), wda_ref[0]) + _dot(hb.astype(BF16), wdb_ref[0])

    @pl.when(jnp.logical_not(valid))
    def _():
        o_ref[...] = jnp.zeros_like(o_ref)


def _ffn_call(tcls, etab, buf, wg, wu, wd):
    rows, w = buf.shape
    ne, d, ff = wg.shape

    def expert(which):
        def index(j, tcls_ref, etab_ref):
            c = jnp.minimum(tcls_ref[j], N_CLASSES - 1)
            return (etab_ref[which * N_CLASSES + c], 0, 0)
        return index

    up = lambda which: pl.BlockSpec((1, d, ff), expert(which))
    down = lambda which: pl.BlockSpec((1, ff, d), expert(which))
    return pl.pallas_call(
        _ffn_kernel,
        grid_spec=pltpu.PrefetchScalarGridSpec(
            num_scalar_prefetch=2,
            grid=(rows // SORT_TILE,),
            in_specs=[pl.BlockSpec((SORT_TILE, w), lambda j, *_: (j, 0)),
                      up(0), up(0), down(0), up(1), up(1), down(1)],
            out_specs=pl.BlockSpec((SORT_TILE, d), lambda j, *_: (j, 0))),
        out_shape=jax.ShapeDtypeStruct((rows, d), F32),
        compiler_params=_cparams("arbitrary"),
        name="moe_ffn",
    )(tcls, etab, buf, wg, wu, wd, wg, wu, wd)


def _combine_kernel(pos_ref, x_ref, mod_ref, ys_hbm, o_ref, ybuf, sem):
    tm, d = x_ref.shape
    i = pl.program_id(0)

    def gather(tile, op):
        slot = tile % 2

        def copy_of_row(base, j):
            return pltpu.make_async_copy(ys_hbm.at[pl.ds(pos_ref[tile * tm + base + j], 1)],
                                         ybuf.at[slot, pl.ds(base + j, 1)], sem.at[slot])

        _row_copies(tm, copy_of_row, op)

    @pl.when(i == 0)
    def _():
        gather(i, "start")

    @pl.when(i + 1 < pl.num_programs(0))
    def _():
        gather(i + 1, "start")

    gather(i, "wait")
    o_ref[...] = x_ref[...] + mod_ref[0, :, 5 * d:6 * d] * ybuf[i % 2]


def _combine_call(pos, x, mod, ys, *, tm):
    t, d = x.shape
    tiles_per_mod = t // mod.shape[0] // tm
    return pl.pallas_call(
        _combine_kernel,
        grid_spec=pltpu.PrefetchScalarGridSpec(
            num_scalar_prefetch=1,
            grid=(t // tm,),
            in_specs=[pl.BlockSpec((tm, d), lambda i, *_: (i, 0)),
                      pl.BlockSpec((1, 1, mod.shape[2]), lambda i, *_: (i // tiles_per_mod, 0, 0)),
                      pl.BlockSpec(memory_space=pl.ANY)],
            out_specs=pl.BlockSpec((tm, d), lambda i, *_: (i, 0)),
            scratch_shapes=[pltpu.VMEM((2, tm, d), F32), pltpu.SemaphoreType.DMA((2,))]),
        out_shape=jax.ShapeDtypeStruct((t, d), F32),
        compiler_params=_cparams("arbitrary"),
        name="moe_combine",
    )(pos, x, mod, ys)


def _moe(streams, wg, wu, wd):
    cls = jnp.concatenate([s[2] for s in streams], axis=0)
    nt, _, tm = cls.shape
    n_sorted_tiles = nt * tm // SORT_TILE + N_CLASSES
    pos, tcls = _rank_call(cls, n_sorted_tiles)
    pos = pos.reshape(nt * tm)
    buf = jnp.zeros((n_sorted_tiles * SORT_TILE, streams[0][1].shape[1]), F32)
    bounds = [0]
    for s in streams:
        bounds.append(bounds[-1] + s[0].shape[0])
        buf = _dispatch_call(pos[bounds[-2]:bounds[-1]], s[1], buf, tm=tm)
    ys = _ffn_call(tcls[0, :n_sorted_tiles], _class_experts(), buf, wg, wu, wd)
    return [_combine_call(pos[bounds[k]:bounds[k + 1]], s[0], s[3], ys, tm=tm)
            for k, s in enumerate(streams)]


def _rope_tables(n_tokens):
    rows = n_tokens // GRID_W
    row_ids = jnp.repeat(jnp.arange(rows), GRID_W).astype(F32)
    col_ids = jnp.tile(jnp.arange(GRID_W), rows).astype(F32)
    half = HEAD_DIM // 2
    inv = ROPE_THETA ** (-jnp.arange(0, half, 2, dtype=F32) / half)
    ang_r = row_ids[:, None] * inv
    ang_c = col_ids[:, None] * inv
    ang = jnp.concatenate([ang_r, ang_r, ang_c, ang_c] * (LANES // HEAD_DIM), axis=-1)
    return jnp.cos(ang), jnp.sin(ang)


def _head_mean_matrix():
    i = lax.broadcasted_iota(jnp.int32, (LANES, LANES), 0) // HEAD_DIM
    j = lax.broadcasted_iota(jnp.int32, (LANES, LANES), 1) // HEAD_DIM
    return jnp.where(i == j, 1.0 / HEAD_DIM, 0.0).astype(BF16)


def kernel(x, c, ctx, c_ctx, w_mod, b_mod, g_norm1, g_norm2, w_in, q_norm, k_norm, hy_short_w, hy_short_b, hy_w1, hy_b1, hy_w2, hy_b2, hy_w3, hy_freq, hy_bias, cf_dw_w, cf_dw_b, cf_ln_g, cf_ln_b, p_attn, p_hyena, p_conv, w_out, w_router, b_router, w_exp_gate, w_exp_up, w_exp_down):
    batch, seq, d = x.shape
    n_ctx = ctx.shape[1]
    depth = w_mod.shape[0]
    hy_w = hy_short_w.shape[2]
    cf_w = 2 * cf_dw_w.shape[2]
    tm = 512

    pad_rows = (-(batch + 1)) % 8
    cc = jnp.concatenate([c, c_ctx[None, :], jnp.zeros((pad_rows, d), F32)], axis=0)
    mods = _mod_call(cc, w_mod, b_mod)

    cos_x, sin_x = _rope_tables(seq)
    cos_c = jnp.ones((tm, LANES), F32)
    sin_c = jnp.zeros((tm, LANES), F32)
    hm = _head_mean_matrix()
    tile2 = lambda g: jnp.tile(g, (1, LANES // HEAD_DIM))
    qg, kg = tile2(q_norm), tile2(k_norm)

    w1p = jnp.pad(hy_w1, ((0, 0), (0, LANES - hy_w1.shape[1]), (0, 0)))
    r3 = lambda a: a[:, None, :]
    tabs_x = _dft_tables(seq)
    tabs_c = _dft_tables(n_ctx)
    filt_args = (w1p, r3(hy_b1), hy_w2, r3(hy_b2), hy_w3, r3(hy_freq))
    kr_x, ki_x, kn_x = _hyena_filter_call(seq, tabs_x, *filt_args)
    kr_c, ki_c, kn_c = _hyena_filter_call(n_ctx, tabs_c, *filt_args)

    bf = lambda a: a.astype(BF16)
    w_in_b, pa_b, ph_b, pc_b, wo_b = bf(w_in), bf(p_attn), bf(p_hyena), bf(p_conv), bf(w_out)
    wg_b, wu_b, wd_b = bf(w_exp_gate), bf(w_exp_up), bf(w_exp_down)
    wr_t = w_router.T
    br = b_router[:, None]

    xl = x.reshape(batch * seq, d)
    xc = ctx.reshape(batch * n_ctx, d)
    for l in range(depth):
        last = l == depth - 1
        mod_x = mods[l, :batch][:, None, :]
        mod_c = mods[l, batch:batch + 1][:, None, :]
        row = lambda a: a[l][None, :]
        streams = [(xl, mod_x, seq, cos_x, sin_x, tabs_x, kr_x, ki_x, kn_x),
                   (xc, mod_c, n_ctx, cos_c, sin_c, tabs_c, kr_c, ki_c, kn_c)]
        proj = [
            _inproj_call(s[0], s[1], row(g_norm1), w_in_b[l], s[3], s[4], qg[l:l + 1], kg[l:l + 1], hm,
                         tm=tm, hy_w=hy_w, cf_w=cf_w)
            for s in streams]
        (q_x, k_x, v_x, zhy_x, zcf_x, zg_x), (q_c, k_c, v_c, zhy_c, zcf_c, zg_c) = proj
        routed = []
        for si, s in enumerate(streams):
            if si == 1 and last:
                break
            xs, mod, n, _, _, tabs, kr, ki, kn = s
            if si == 0:
                oa = _attn_call(q_x, [(k_x, v_x), (k_c, v_c)], batch=batch, tq=512)
                zhy, zcf, zg = zhy_x, zcf_x, zg_x
            else:
                oa = _attn_call(q_c, [(k_c, v_c)], batch=batch, tq=n_ctx)
                zhy, zcf, zg = zhy_c, zcf_c, zg_c
            oh = _hyena_call(zhy, hy_short_w[l], row(hy_short_b), tabs[0], tabs[2], kr[l], ki[l], kn[l],
                             row(hy_bias), n=n)
            oc = _conf_call(zcf, cf_dw_w[l], row(cf_dw_b), row(cf_ln_g), row(cf_ln_b), n=n)
            x1, hx, cls = _merge_call(oa, oh, oc, zg, xs, mod, row(g_norm2), pa_b[l], ph_b[l], pc_b[l],
                                      wo_b[l], wr_t, br, tm=tm)
            routed.append((x1, hx, cls, mod))
        new = _moe(routed, wg_b[l], wu_b[l], wd_b[l])
        xl = new[0]
        if not last:
            xc = new[1]
    return xl.reshape(batch, seq, d)
```

```python
import functools
import math

import jax
import jax.numpy as jnp
from jax import lax
from jax.experimental import pallas as pl
from jax.experimental.pallas import tpu as pltpu

F32 = jnp.float32
BF16 = jnp.bfloat16

NORM_EPS = 1e-6
N_MOD = 6
HEAD_DIM = 64
N_Q_HEADS = 8
N_KV_HEADS = 2
ATTN_W = N_Q_HEADS * HEAD_DIM
KV_W = N_KV_HEADS * HEAD_DIM
GRID_W = 64
ROPE_THETA = 10000.0
HYENA_BANDS = 16
HYENA_FAST_DECAY = 0.3
HYENA_SLOW_DECAY = 1.5
HYENA_TARGET = 1e-2
CONF_K = 31
N_EXPERTS = 16
N_GROUPS = 4
GROUP_SIZE = N_EXPERTS // N_GROUPS

Q_SCALE = math.log2(math.e) / math.sqrt(HEAD_DIM)

LANES = 128
SUBLANES = 8
VMEM_LIMIT = 56 * 1024 * 1024


def _cparams(*sem):
    return pltpu.CompilerParams(dimension_semantics=sem, vmem_limit_bytes=VMEM_LIMIT)


def _dot(a, b):
    return jnp.dot(a, b, preferred_element_type=F32)


def _split(a):
    hi = a.astype(BF16)
    lo = (a - hi.astype(F32)).astype(BF16)
    return hi, lo


def _dot3(a, b):
    ah, al = _split(a)
    bh, bl = _split(b)
    return _dot(ah, bh) + (_dot(al, bh) + _dot(ah, bl))


def _silu(x):
    return x * jax.nn.sigmoid(x)


def _mod_kernel(c_ref, w_ref, b_ref, o_ref):
    o_ref[0] = _dot3(_silu(c_ref[...]), w_ref[0]) + b_ref[0]


def _mod_call(cc, w_mod, b_mod):
    depth, d, n = w_mod.shape
    r = cc.shape[0]
    tn = 1536
    return pl.pallas_call(
        _mod_kernel,
        grid=(depth, n // tn),
        in_specs=[pl.BlockSpec((r, d), lambda l, j: (0, 0)),
                  pl.BlockSpec((1, d, tn), lambda l, j: (l, 0, j)),
                  pl.BlockSpec((1, 1, tn), lambda l, j: (l, 0, j))],
        out_specs=pl.BlockSpec((1, r, tn), lambda l, j: (l, 0, j)),
        out_shape=jax.ShapeDtypeStruct((depth, r, n), F32),
        compiler_params=_cparams("arbitrary", "arbitrary"),
        name="adaln_mod",
    )(cc, w_mod, b_mod.reshape(depth, 1, n))


def _row_copies(n_groups, copy_of_row, op):
    def group(g, carry):
        for j in range(SUBLANES):
            cp = copy_of_row(g, j)
            if op == "start":
                cp.start(priority=j % 2)
            else:
                cp.wait()
        return carry

    lax.fori_loop(0, n_groups, group, 0)


def _grouped(a):
    return a.reshape(a.shape[0] // SUBLANES, SUBLANES, a.shape[1])


def _row_at(ref, p):
    shift = SUBLANES.bit_length() - 1
    return ref.at[lax.shift_right_logical(p, shift), pl.ds(p & (SUBLANES - 1), 1)]


def _inproj_kernel(*refs, pending):
    if pending:
        (pos_ref, x_ref, mod_ref, g_ref, w_ref, cos_ref, sin_ref, qg_ref, kg_ref, hm_ref,
         modp_ref, ys_hbm, q_ref, k_ref, v_ref, zhy_ref, zcf_ref, zg_ref, xo_ref, ybuf, sem) = refs
    else:
        (x_ref, mod_ref, g_ref, w_ref, cos_ref, sin_ref, qg_ref, kg_ref, hm_ref,
         q_ref, k_ref, v_ref, zhy_ref, zcf_ref, zg_ref) = refs
    tm, d = x_ref.shape
    x = x_ref[...]
    issue_next = lambda part, parts: None
    if pending:
        groups = tm // SUBLANES
        i = pl.program_id(0)
        last = pl.num_programs(0) - 1

        def copy_of_row(tile, slot):
            def build(g, j):
                p = pos_ref[tile * tm + g * SUBLANES + j]
                return pltpu.make_async_copy(_row_at(ys_hbm, p), ybuf.at[slot, g, pl.ds(j, 1)],
                                             sem.at[slot])
            return build

        @pl.when(i == 0)
        def _():
            _row_copies(groups, copy_of_row(i, 0), "start")

        nxt = copy_of_row(jnp.minimum(i + 1, last), (i + 1) % 2)

        def issue_next(part, parts):
            for g in range(groups * part // parts, groups * (part + 1) // parts):
                for j in range(SUBLANES):
                    nxt(g, j).start(priority=j % 2)

        _row_copies(groups, copy_of_row(i, i % 2), "wait")
        x = x + modp_ref[0, :, 5 * d:6 * d] * ybuf[i % 2].reshape(tm, d)
        xo_ref[...] = x

    ms = jnp.mean(x * x, axis=-1, keepdims=True)
    y = x * lax.rsqrt(ms + NORM_EPS) * g_ref[...]
    sh = mod_ref[0, :, 0:d]
    sc = mod_ref[0, :, d:2 * d]
    h = (y * (1.0 + sc) + sh).astype(BF16)

    cos = cos_ref[...]
    sin = sin_ref[...]
    hm = hm_ref[...]
    lane = lax.broadcasted_iota(jnp.int32, (1, LANES), 1)
    first_quarter = (lane % 32) < 16

    def norm_rope(z, gain, scale):
        hi, lo = _split(z * z)
        msq = _dot(hi, hm) + _dot(lo, hm)
        zn = z * lax.rsqrt(msq + NORM_EPS) * gain
        rot = jnp.where(first_quarter, -pltpu.roll(zn, LANES - 16, 1), pltpu.roll(zn, 16, 1))
        return (zn * cos + rot * sin) * scale

    chunk = 1024
    n_gate_chunks = zg_ref.shape[1] // chunk
    parts = 4 + n_gate_chunks
    zq = _dot(h, w_ref[:, 0:ATTN_W])
    for cb in range(ATTN_W // LANES):
        sl = slice(cb * LANES, (cb + 1) * LANES)
        q_ref[:, sl] = norm_rope(zq[:, sl], qg_ref[...], Q_SCALE).astype(BF16)
    issue_next(0, parts)
    zkv = _dot(h, w_ref[:, ATTN_W:ATTN_W + 2 * KV_W])
    k_ref[...] = norm_rope(zkv[:, 0:KV_W], kg_ref[...], 1.0).astype(BF16)
    v_ref[...] = zkv[:, KV_W:2 * KV_W].astype(BF16)
    issue_next(1, parts)
    off = ATTN_W + 2 * KV_W
    for part, ref in ((2, zhy_ref), (3, zcf_ref)):
        w = ref.shape[1]
        ref[...] = _dot(h, w_ref[:, off:off + w])
        off += w
        issue_next(part, parts)
    for cb in range(n_gate_chunks):
        zg_ref[:, cb * chunk:(cb + 1) * chunk] = _dot(
            h, w_ref[:, off + cb * chunk:off + (cb + 1) * chunk])
        issue_next(4 + cb, parts)

    if pending:
        @pl.when(i == last)
        def _():
            _row_copies(groups, nxt, "wait")


def _inproj_call(x, mod, g1, w_in, cos, sin, qg, kg, hm, pending=None, *, tm, hy_w, cf_w):
    t, d = x.shape
    tiles_per_mod = t // mod.shape[0] // tm
    n = w_in.shape[1]
    g_w = n - ATTN_W - 2 * KV_W - hy_w - cf_w
    n_pos = cos.shape[0] // tm
    row = lambda i, *_: (i, 0)
    const = lambda i, *_: (0, 0)
    mod_spec = pl.BlockSpec((1, 1, mod.shape[2]), lambda i, *_: (i // tiles_per_mod, 0, 0))
    in_specs = [pl.BlockSpec((tm, d), row), mod_spec,
                pl.BlockSpec((1, d), const),
                pl.BlockSpec((d, n), const),
                pl.BlockSpec((tm, LANES), lambda i, *_: (i % n_pos, 0)),
                pl.BlockSpec((tm, LANES), lambda i, *_: (i % n_pos, 0)),
                pl.BlockSpec((1, LANES), const),
                pl.BlockSpec((1, LANES), const),
                pl.BlockSpec((LANES, LANES), const)]
    out_specs = [pl.BlockSpec((tm, ATTN_W), row),
                 pl.BlockSpec((tm, KV_W), row),
                 pl.BlockSpec((tm, KV_W), row),
                 pl.BlockSpec((tm, hy_w), row),
                 pl.BlockSpec((tm, cf_w), row),
                 pl.BlockSpec((tm, g_w), row)]
    out_shape = [jax.ShapeDtypeStruct((t, ATTN_W), BF16),
                 jax.ShapeDtypeStruct((t, KV_W), BF16),
                 jax.ShapeDtypeStruct((t, KV_W), BF16),
                 jax.ShapeDtypeStruct((t, hy_w), F32),
                 jax.ShapeDtypeStruct((t, cf_w), F32),
                 jax.ShapeDtypeStruct((t, g_w), F32)]
    args = [x, mod, g1, w_in, cos, sin, qg, kg, hm]
    scratch = []
    if pending is not None:
        pos, mod_prev, ys = pending
        in_specs += [mod_spec, pl.BlockSpec(memory_space=pl.ANY)]
        out_specs.append(pl.BlockSpec((tm, d), row))
        out_shape.append(jax.ShapeDtypeStruct((t, d), F32))
        args = [pos] + args + [mod_prev, _grouped(ys)]
        scratch = [pltpu.VMEM((2, tm // SUBLANES, SUBLANES, d), F32), pltpu.SemaphoreType.DMA((2,))]
    return pl.pallas_call(
        functools.partial(_inproj_kernel, pending=pending is not None),
        grid_spec=pltpu.PrefetchScalarGridSpec(
            num_scalar_prefetch=0 if pending is None else 1,
            grid=(t // tm,),
            in_specs=in_specs, out_specs=out_specs, scratch_shapes=scratch),
        out_shape=out_shape,
        compiler_params=_cparams("arbitrary"),
        name="in_proj",
    )(*args)


def _attn_kernel(*refs, n_src):
    q_ref = refs[0]
    kv_refs = refs[1:1 + 2 * n_src]
    o_ref = refs[1 + 2 * n_src]
    kk_ref, vv_ref = refs[2 + 2 * n_src:]
    h = pl.program_id(1)
    lane = lax.broadcasted_iota(jnp.int32, (1, LANES), 1)
    lo_half = lane < HEAD_DIM

    @pl.when(pl.program_id(2) == 0)
    def _():
        keep = (lane // HEAD_DIM) == h
        off = 0
        for s in range(n_src):
            n = kv_refs[2 * s].shape[0]
            k = kv_refs[2 * s][...].astype(F32)
            kk_ref[off:off + n, :] = jnp.where(keep, k, pltpu.roll(k, HEAD_DIM, 1)).astype(BF16)
            v = kv_refs[2 * s + 1][...].astype(F32)
            v = jnp.where(keep, v, pltpu.roll(v, HEAD_DIM, 1))
            vv_ref[off:off + n, :] = jnp.where(lo_half, v, 1.0).astype(BF16)
            off += n

    kk = kk_ref[...]
    vv = vv_ref[...]
    n_heads = 2 * (q_ref.shape[1] // LANES)

    def scores(hd):
        q2 = q_ref[:, (hd // 2) * LANES:(hd // 2 + 1) * LANES]
        msk = lo_half if hd % 2 == 0 else jnp.logical_not(lo_half)
        qm = jnp.where(msk, q2, jnp.zeros_like(q2))
        return lax.dot_general(qm, kk, (((1,), (1,)), ((), ())), preferred_element_type=F32)

    def probs(s):
        return jnp.exp2(s - jnp.max(s, axis=-1, keepdims=True)).astype(BF16)

    def values(p):
        o = _dot(p, vv)
        return o / o[:, HEAD_DIM:HEAD_DIM + 1]

    s_all = {0: scores(0)}
    outs = {}
    p_prev = None
    for hd in range(n_heads):
        if hd + 1 < n_heads:
            s_all[hd + 1] = scores(hd + 1)
        p_cur = probs(s_all.pop(hd))
        if p_prev is not None:
            outs[hd - 1] = values(p_prev)
        p_prev = p_cur
    outs[n_heads - 1] = values(p_prev)
    for j in range(n_heads // 2):
        pair = jnp.where(lo_half, outs[2 * j], pltpu.roll(outs[2 * j + 1], HEAD_DIM, 1))
        o_ref[:, j * LANES:(j + 1) * LANES] = pair.astype(BF16)


def _attn_call(q, kvs, *, batch, tq):
    t = q.shape[0]
    nq = t // batch // tq
    gw = ATTN_W // N_KV_HEADS
    in_specs = [pl.BlockSpec((tq, gw), lambda b, h, i: (b * nq + i, h))]
    args = [q]
    nk = 0
    for k, v in kvs:
        n = k.shape[0] // batch
        nk += n
        for a in (k, v):
            in_specs.append(pl.BlockSpec((n, KV_W), lambda b, h, i: (b, 0)))
            args.append(a)
    return pl.pallas_call(
        functools.partial(_attn_kernel, n_src=len(kvs)),
        grid=(batch, N_KV_HEADS, nq),
        in_specs=in_specs,
        out_specs=pl.BlockSpec((tq, gw), lambda b, h, i: (b * nq + i, h)),
        out_shape=jax.ShapeDtypeStruct((t, ATTN_W), BF16),
        scratch_shapes=[pltpu.VMEM((nk, LANES), BF16), pltpu.VMEM((nk, LANES), BF16)],
        compiler_params=_cparams("arbitrary", "arbitrary", "arbitrary"),
        name="attention",
    )(*args)


def _hy_time_kernel(feat_ref, w1_ref, b1_ref, w2_ref, b2_ref, w3_ref, fq_ref, dec_ref,
                    sp_ref, sm_ref, kn_ref):
    n = feat_ref.shape[0]
    fq = fq_ref[0]
    h = jnp.sin(fq * (_dot3(feat_ref[...], w1_ref[0]) + b1_ref[0]))
    h = jnp.sin(fq * (_dot3(h, w2_ref[0]) + b2_ref[0]))
    h = _dot3(h, w3_ref[0]) * dec_ref[...]
    h = h / jnp.sum(jnp.abs(h), axis=0, keepdims=True)
    c = h.shape[1] // 2
    row = lax.broadcasted_iota(jnp.int32, (n, 1), 0)
    hf = h[:, 0:c]
    hb = jnp.where(row == 0, 0.0, h[:, c:2 * c])
    sp = hf + hb
    sp_ref[0] = sp
    sm_ref[0] = hf - hb
    sgn = (1 - 2 * (row & 1)).astype(F32)
    kn_ref[0] = jnp.sum(sp * sgn, axis=0, keepdims=True) * (0.5 / n)


def _hy_spec_kernel(fch_ref, fcl_ref, fsh_ref, fsl_ref, sp_ref, sm_ref, kr_ref, ki_ref, *, n):
    tk = fch_ref.shape[0]
    k = pl.program_id(1) * tk + lax.broadcasted_iota(jnp.int32, (tk, 1), 0)
    scl = jnp.where(k == 0, 0.5 / n, 1.0 / n)
    sph, spl = _split(sp_ref[0])
    smh, sml = _split(sm_ref[0])
    kr = _dot(fch_ref[...], sph) + (_dot(fcl_ref[...], sph) + _dot(fch_ref[...], spl))
    ki = _dot(fsh_ref[...], smh) + (_dot(fsl_ref[...], smh) + _dot(fsh_ref[...], sml))
    kr_ref[0] = kr * scl
    ki_ref[0] = ki * scl


def _dft_tables(n):
    k = lax.broadcasted_iota(jnp.int32, (n, n), 0)
    t = lax.broadcasted_iota(jnp.int32, (n, n), 1)
    ang = ((k * t) % (2 * n)).astype(F32) * (math.pi / n)
    return _split(jnp.cos(ang)) + _split(-jnp.sin(ang))


def _hyena_features(n, c):
    t = jnp.linspace(0.0, 1.0, n, dtype=F32)[:, None]
    w = 2.0 * math.pi * jnp.arange(n, dtype=F32)[:, None] / n
    f = jnp.linspace(1e-4, HYENA_BANDS - 1, HYENA_BANDS, dtype=F32)[None, :]
    feat = jnp.concatenate([t, jnp.cos(f * w), -jnp.sin(f * w)], axis=-1)
    feat = jnp.pad(feat, ((0, 0), (0, LANES - feat.shape[1])))
    max_decay = math.log(HYENA_TARGET) / HYENA_FAST_DECAY
    min_decay = math.log(HYENA_TARGET) / HYENA_SLOW_DECAY
    deltas = jnp.tile(jnp.linspace(min_decay, max_decay, c, dtype=F32), 2)
    return feat, jnp.exp(-t * jnp.abs(deltas))


def _hyena_filter_call(n, tabs, w1p, b1, w2, b2, w3, fq):
    depth = w1p.shape[0]
    ffn = w2.shape[1]
    c = w3.shape[2] // 2
    feat, dec = _hyena_features(n, c)
    lay = lambda l: (l, 0, 0)
    const = lambda l: (0, 0)
    sp, sm, kn = pl.pallas_call(
        _hy_time_kernel,
        grid=(depth,),
        in_specs=[pl.BlockSpec((n, LANES), const),
                  pl.BlockSpec((1, LANES, ffn), lay), pl.BlockSpec((1, 1, ffn), lay),
                  pl.BlockSpec((1, ffn, ffn), lay), pl.BlockSpec((1, 1, ffn), lay),
                  pl.BlockSpec((1, ffn, 2 * c), lay), pl.BlockSpec((1, 1, ffn), lay),
                  pl.BlockSpec((n, 2 * c), const)],
        out_specs=[pl.BlockSpec((1, n, c), lay), pl.BlockSpec((1, n, c), lay),
                   pl.BlockSpec((1, 1, c), lay)],
        out_shape=[jax.ShapeDtypeStruct((depth, n, c), F32), jax.ShapeDtypeStruct((depth, n, c), F32),
                   jax.ShapeDtypeStruct((depth, 1, c), F32)],
        compiler_params=_cparams("arbitrary"),
        name="hyena_filter_time",
    )(feat, w1p, b1, w2, b2, w3, fq, dec)
    tk = min(n, 512)
    mat = pl.BlockSpec((tk, n), lambda l, j: (j, 0))
    vec = pl.BlockSpec((1, n, c), lambda l, j: (l, 0, 0))
    out = pl.BlockSpec((1, tk, c), lambda l, j: (l, j, 0))
    kr, ki = pl.pallas_call(
        functools.partial(_hy_spec_kernel, n=n),
        grid=(depth, n // tk),
        in_specs=[mat, mat, mat, mat, vec, vec],
        out_specs=[out, out],
        out_shape=[jax.ShapeDtypeStruct((depth, n, c), F32)] * 2,
        compiler_params=_cparams("arbitrary", "arbitrary"),
        name="hyena_filter_spectrum",
    )(*tabs, sp, sm)
    return kr, ki, kn


HYENA_ROW_CHUNK = 512


def _hyena_kernel(z_ref, sw_ref, sb_ref, fc_ref, fs_ref, kr_ref, ki_ref, kn_ref, db_ref, o_ref):
    n = z_ref.shape[0]
    c = o_ref.shape[1]
    z = z_ref[...]
    row = lax.broadcasted_iota(jnp.int32, (n, 1), 0)
    z_prev = jnp.where(row == 0, 0.0, pltpu.roll(z, 1, 0))
    z_next = jnp.where(row == n - 1, 0.0, pltpu.roll(z, n - 1, 0))
    zc = z_prev * sw_ref[0:1, :] + z * sw_ref[1:2, :] + z_next * sw_ref[2:3, :] + sb_ref[...]
    x0 = zc[:, 0:c]
    u = zc[:, 2 * c:3 * c] * zc[:, c:2 * c]
    ub = u.astype(BF16)
    sgn = (1 - 2 * (row & 1)).astype(F32)
    nyq = jnp.sum(u * sgn, axis=0, keepdims=True) * kn_ref[...]
    chunk = min(n, HYENA_ROW_CHUNK)
    chunks = [slice(r, r + chunk) for r in range(0, n, chunk)]
    pr, pi = [], []
    for rs in chunks:
        ur = _dot(fc_ref[rs, :], ub)
        ui = _dot(fs_ref[rs, :], ub)
        kr = kr_ref[rs, :]
        ki = ki_ref[rs, :]
        pr.append((ur * kr - ui * ki).astype(BF16))
        pi.append((ur * ki + ui * kr).astype(BF16))
    pr = jnp.concatenate(pr, axis=0)
    pi = jnp.concatenate(pi, axis=0)
    for rs in chunks:
        y = _dot(fc_ref[rs, :], pr) + _dot(fs_ref[rs, :], pi) + sgn[rs, :] * nyq
        o_ref[rs, :] = ((y + u[rs, :] * db_ref[...]) * x0[rs, :]).astype(BF16)


def _hyena_call(z, sw, sb, fc, fs, kr, ki, kn, db, *, n):
    t, w = z.shape
    c = w // 3
    const = lambda b: (0, 0)
    return pl.pallas_call(
        _hyena_kernel,
        grid=(t // n,),
        in_specs=[pl.BlockSpec((n, w), lambda b: (b, 0)),
                  pl.BlockSpec((3, w), const), pl.BlockSpec((1, w), const),
                  pl.BlockSpec((n, n), const), pl.BlockSpec((n, n), const),
                  pl.BlockSpec((n, c), const), pl.BlockSpec((n, c), const),
                  pl.BlockSpec((1, c), const), pl.BlockSpec((1, c), const)],
        out_specs=pl.BlockSpec((n, c), lambda b: (b, 0)),
        out_shape=jax.ShapeDtypeStruct((t, c), BF16),
        compiler_params=_cparams("arbitrary"),
        name="hyena_conv",
    )(z, sw, sb, fc, fs, kr, ki, kn, db)


CONF_PAD = 16


def _conf_kernel(z_ref, w_ref, b_ref, g_ref, beta_ref, o_ref, pad_ref):
    n = z_ref.shape[0]
    c = o_ref.shape[1]
    u = z_ref[:, 0:c] * jax.nn.sigmoid(z_ref[:, c:2 * c])
    zeros = jnp.zeros((CONF_PAD, c), F32)
    pad_ref[0:CONF_PAD, :] = zeros
    pad_ref[CONF_PAD + n:2 * CONF_PAD + n, :] = zeros
    pad_ref[CONF_PAD:CONF_PAD + n, :] = u
    first = CONF_PAD - (CONF_K - 1) // 2
    acc = b_ref[...]
    for r in range(SUBLANES):
        part = None
        for off in range(r, first + CONF_K, SUBLANES):
            j = off - first
            if j < 0:
                continue
            term = w_ref[j:j + 1, :] * pad_ref[off - r:off - r + n + SUBLANES, :]
            part = term if part is None else part + term
        acc = acc + part[r:r + n, :]
    mu = jnp.mean(acc, axis=-1, keepdims=True)
    dlt = acc - mu
    var = jnp.mean(dlt * dlt, axis=-1, keepdims=True)
    y = dlt * lax.rsqrt(var + NORM_EPS) * g_ref[...] + beta_ref[...]
    o_ref[...] = _silu(y).astype(BF16)


def _conf_call(z, w, b, g, beta, *, n):
    t, w2 = z.shape
    c = w2 // 2
    const = lambda i: (0, 0)
    return pl.pallas_call(
        _conf_kernel,
        grid=(t // n,),
        in_specs=[pl.BlockSpec((n, w2), lambda i: (i, 0)),
                  pl.BlockSpec((CONF_K, c), const), pl.BlockSpec((1, c), const),
                  pl.BlockSpec((1, c), const), pl.BlockSpec((1, c), const)],
        out_specs=pl.BlockSpec((n, c), lambda i: (i, 0)),
        out_shape=jax.ShapeDtypeStruct((t, c), BF16),
        scratch_shapes=[pltpu.VMEM((n + 2 * CONF_PAD, c), F32)],
        compiler_params=_cparams("arbitrary"),
        name="conformer_conv",
    )(z, w, b, g, beta)


def _route(sel, s):
    grp_score = []
    for g in range(N_GROUPS):
        r = sel[g * GROUP_SIZE:(g + 1) * GROUP_SIZE]
        best = None
        for i in range(GROUP_SIZE):
            for j in range(i + 1, GROUP_SIZE):
                pair = r[i] + r[j]
                best = pair if best is None else jnp.maximum(best, pair)
        grp_score.append(best)
    gbest = grp_score[0]
    gidx = jnp.zeros_like(gbest, dtype=jnp.int32)
    for g in range(1, N_GROUPS):
        better = grp_score[g] > gbest
        gidx = jnp.where(better, g, gidx)
        gbest = jnp.where(better, grp_score[g], gbest)
    neg = jnp.full_like(gbest, -jnp.inf)
    masked = [jnp.where(gidx == e // GROUP_SIZE, sel[e], neg) for e in range(N_EXPERTS)]

    def argmax_first(vals):
        best = vals[0]
        idx = jnp.zeros_like(gidx)
        for e in range(1, N_EXPERTS):
            better = vals[e] > best
            idx = jnp.where(better, e, idx)
            best = jnp.where(better, vals[e], best)
        return idx

    i1 = argmax_first(masked)
    i2 = argmax_first([jnp.where(i1 == e, neg, masked[e]) for e in range(N_EXPERTS)])
    zero = jnp.zeros_like(gbest)
    w1 = sum(jnp.where(i1 == e, s[e], zero) for e in range(N_EXPERTS))
    w2 = sum(jnp.where(i2 == e, s[e], zero) for e in range(N_EXPERTS))
    tot = w1 + w2
    w1, w2 = w1 / tot, w2 / tot
    swap = i2 < i1
    a = jnp.where(swap, i2, i1) - GROUP_SIZE * gidx
    b = jnp.where(swap, i1, i2) - GROUP_SIZE * gidx
    cls = PAIRS_PER_GROUP * gidx + ((a * (2 * GROUP_SIZE - 1 - a)) >> 1) + (b - a - 1)
    return cls, jnp.where(swap, w2, w1), jnp.where(swap, w1, w2)


MERGE_ROW_CHUNK = 512


def _merge_kernel(oa_ref, oh_ref, oc_ref, zg_ref, x_ref, mod_ref, g2_ref,
                  pa_ref, ph_ref, pc_ref, wo_ref, wr_ref, br_ref,
                  xo_ref, hx_ref, cls_ref):
    d = x_ref.shape[1]
    wh, wl = _split(wr_ref[...])
    for r0 in range(0, x_ref.shape[0], MERGE_ROW_CHUNK):
        rs = slice(r0, r0 + MERGE_ROW_CHUNK)
        merged = jax.nn.sigmoid(zg_ref[rs, 0:d]) * _dot(oa_ref[rs, :], pa_ref[...])
        merged += jax.nn.sigmoid(zg_ref[rs, d:2 * d]) * _dot(oh_ref[rs, :], ph_ref[...])
        merged += jax.nn.sigmoid(zg_ref[rs, 2 * d:3 * d]) * _dot(oc_ref[rs, :], pc_ref[...])
        y = _dot(merged.astype(BF16), wo_ref[...])
        x = x_ref[rs, :] + mod_ref[0, :, 2 * d:3 * d] * y
        xo_ref[rs, :] = x
        ms = jnp.mean(x * x, axis=-1, keepdims=True)
        h2 = x * lax.rsqrt(ms + NORM_EPS) * g2_ref[...]
        h2 = h2 * (1.0 + mod_ref[0, :, 4 * d:5 * d]) + mod_ref[0, :, 3 * d:4 * d]
        hx_ref[rs, 0:d] = h2
        hh, hl = _split(h2)
        nt = (((1,), (1,)), ((), ()))
        logits = (lax.dot_general(wh, hh, nt, preferred_element_type=F32)
                  + (lax.dot_general(wh, hl, nt, preferred_element_type=F32)
                     + lax.dot_general(wl, hh, nt, preferred_element_type=F32)))
        s = jax.nn.sigmoid(logits)
        sel = s + br_ref[...]
        cls, w_lo, w_hi = _route([sel[e:e + 1, :] for e in range(N_EXPERTS)],
                                 [s[e:e + 1, :] for e in range(N_EXPERTS)])
        cls_ref[0, :, rs] = cls
        sub = lax.broadcasted_iota(jnp.int32, (LANES, MERGE_ROW_CHUNK), 0)
        rows = jnp.where(sub == 0, w_lo, jnp.where(sub == 1, w_hi, 0.0))
        hx_ref[rs, d:d + LANES] = rows.T


def _merge_call(oa, oh, oc, zg, x, mod, g2, pa, ph, pc, wo, wr_t, br, *, tm):
    t, d = x.shape
    tiles_per_mod = t // mod.shape[0] // tm
    row = lambda i: (i, 0)
    const = lambda i: (0, 0)
    full = lambda a: pl.BlockSpec(a.shape, const)
    return pl.pallas_call(
        _merge_kernel,
        grid=(t // tm,),
        in_specs=[pl.BlockSpec((tm, oa.shape[1]), row), pl.BlockSpec((tm, oh.shape[1]), row),
                  pl.BlockSpec((tm, oc.shape[1]), row), pl.BlockSpec((tm, zg.shape[1]), row),
                  pl.BlockSpec((tm, d), row),
                  pl.BlockSpec((1, 1, mod.shape[2]), lambda i: (i // tiles_per_mod, 0, 0)),
                  full(g2), full(pa), full(ph), full(pc), full(wo), full(wr_t), full(br)],
        out_specs=[pl.BlockSpec((tm, d), row), pl.BlockSpec((tm, d + LANES), row),
                   pl.BlockSpec((1, 1, tm), lambda i: (i, 0, 0))],
        out_shape=[jax.ShapeDtypeStruct((t, d), F32), jax.ShapeDtypeStruct((t, d + LANES), F32),
                   jax.ShapeDtypeStruct((t // tm, 1, tm), jnp.int32)],
        compiler_params=_cparams("arbitrary"),
        name="merge_router",
    )(oa, oh, oc, zg, x, mod, g2, pa, ph, pc, wo, wr_t, br)


PAIRS_PER_GROUP = GROUP_SIZE * (GROUP_SIZE - 1) // 2
N_CLASSES = N_GROUPS * PAIRS_PER_GROUP
SORT_TILE = 512


def _class_experts():
    lo, hi = [], []
    for g in range(N_GROUPS):
        for a in range(GROUP_SIZE):
            for b in range(a + 1, GROUP_SIZE):
                lo.append(g * GROUP_SIZE + a)
                hi.append(g * GROUP_SIZE + b)
    return jnp.asarray(lo + hi, dtype=jnp.int32)


def _col(row):
    return jnp.broadcast_to(row, (LANES, row.shape[1])).T[:, 0:1]


def _row(col):
    return jnp.broadcast_to(col, (col.shape[0], LANES)).T[0:1, :]


def _rank_kernel(cls_ref, pos_ref, tcls_ref, cnt_ref, base_ref):
    phase = pl.program_id(0)
    i = pl.program_id(1)
    tm = cls_ref.shape[2]
    cls_col = _col(cls_ref[0].astype(F32)).astype(jnp.int32)
    lane = lax.broadcasted_iota(jnp.int32, (1, LANES), 1)
    onehot = (cls_col == lane).astype(F32)
    counts = jnp.sum(onehot, axis=0, keepdims=True)

    @pl.when(phase == 0)
    def _():
        @pl.when(i == 0)
        def _():
            cnt_ref[...] = jnp.zeros_like(cnt_ref)
        cnt_ref[...] += counts

    @pl.when(phase == 1)
    def _():
        @pl.when(i == 0)
        def _():
            shift = SORT_TILE.bit_length() - 1
            ntile = lax.shift_right_logical(cnt_ref[...].astype(jnp.int32) + (SORT_TILE - 1), shift)
            ntile = ntile.astype(F32)
            r = lax.broadcasted_iota(jnp.int32, (LANES, LANES), 0)
            c = lax.broadcasted_iota(jnp.int32, (LANES, LANES), 1)
            start = _dot3(jnp.broadcast_to(ntile, (8, LANES)), (r < c).astype(F32))[0:1, :]
            base_ref[...] = start * SORT_TILE
            end_col = _col(start + ntile)
            tile = lax.broadcasted_iota(jnp.int32, (1, tcls_ref.shape[1]), 1).astype(F32)
            tcls_ref[...] = jnp.sum((end_col <= tile).astype(F32), axis=0,
                                    keepdims=True).astype(jnp.int32)
        t = lax.broadcasted_iota(jnp.int32, (tm, tm), 0)
        s = lax.broadcasted_iota(jnp.int32, (tm, tm), 1)
        earlier = _dot((s < t).astype(BF16), onehot.astype(BF16))
        pos_col = jnp.sum(onehot * (earlier + base_ref[...]), axis=1, keepdims=True)
        pos_ref[0] = _row(pos_col).astype(jnp.int32)
        base_ref[...] += counts


def _rank_call(cls, n_sorted_tiles):
    nt, _, tm = cls.shape
    nsp = -(-n_sorted_tiles // LANES) * LANES
    return pl.pallas_call(
        _rank_kernel,
        grid=(2, nt),
        in_specs=[pl.BlockSpec((1, 1, tm), lambda p, i: (i, 0, 0))],
        out_specs=[pl.BlockSpec((1, 1, tm), lambda p, i: (i * p, 0, 0)),
                   pl.BlockSpec((1, nsp), lambda p, i: (0, 0))],
        out_shape=[jax.ShapeDtypeStruct((nt, 1, tm), jnp.int32),
                   jax.ShapeDtypeStruct((1, nsp), jnp.int32)],
        scratch_shapes=[pltpu.VMEM((1, LANES), F32), pltpu.VMEM((1, LANES), F32)],
        compiler_params=_cparams("arbitrary", "arbitrary"),
        name="moe_rank",
    )(cls)


def _dispatch_kernel(pos_ref, hx_ref, buf_in, buf_out, sem):
    del buf_in
    groups = hx_ref.shape[0]
    first = pl.program_id(0) * groups * SUBLANES

    def copy_of_row(g, j):
        p = pos_ref[first + g * SUBLANES + j]
        return pltpu.make_async_copy(hx_ref.at[g, pl.ds(j, 1)], _row_at(buf_out, p), sem)

    _row_copies(groups, copy_of_row, "start")
    _row_copies(groups, copy_of_row, "wait")


def _dispatch_call(pos, hx, buf, *, tm):
    t, w = hx.shape
    any_spec = pl.BlockSpec(memory_space=pl.ANY)
    out = pl.pallas_call(
        _dispatch_kernel,
        grid_spec=pltpu.PrefetchScalarGridSpec(
            num_scalar_prefetch=1,
            grid=(t // tm,),
            in_specs=[pl.BlockSpec((tm // SUBLANES, SUBLANES, w), lambda i, *_: (i, 0, 0)), any_spec],
            out_specs=any_spec,
            scratch_shapes=[pltpu.SemaphoreType.DMA]),
        out_shape=jax.ShapeDtypeStruct((buf.shape[0] // SUBLANES, SUBLANES, w), buf.dtype),
        input_output_aliases={2: 0},
        compiler_params=_cparams("arbitrary"),
        name="moe_dispatch",
    )(pos, _grouped(hx), _grouped(buf))
    return out.reshape(buf.shape)


def _ffn_kernel(tcls_ref, etab_ref, xs_ref, wga_ref, wua_ref, wda_ref, wgb_ref, wub_ref, wdb_ref, o_ref):
    del etab_ref
    d = o_ref.shape[1]
    valid = tcls_ref[pl.program_id(0)] < N_CLASSES

    @pl.when(valid)
    def _():
        x = xs_ref[:, 0:d].astype(BF16)
        ha = _silu(_dot(x, wga_ref[0])) * _dot(x, wua_ref[0]) * xs_ref[:, d:d + 1]
        hb = _silu(_dot(x, wgb_ref[0])) * _dot(x, wub_ref[0]) * xs_ref[:, d + 1:d + 2]
        o_ref[...] = _dot(ha.astype(BF16), wda_ref[0]) + _dot(hb.astype(BF16), wdb_ref[0])

    @pl.when(jnp.logical_not(valid))
    def _():
        o_ref[...] = jnp.zeros_like(o_ref)


def _ffn_call(tcls, etab, buf, wg, wu, wd):
    rows, w = buf.shape
    ne, d, ff = wg.shape

    def expert(which):
        def index(j, tcls_ref, etab_ref):
            c = jnp.minimum(tcls_ref[j], N_CLASSES - 1)
            return (etab_ref[which * N_CLASSES + c], 0, 0)
        return index

    up = lambda which: pl.BlockSpec((1, d, ff), expert(which))
    down = lambda which: pl.BlockSpec((1, ff, d), expert(which))
    return pl.pallas_call(
        _ffn_kernel,
        grid_spec=pltpu.PrefetchScalarGridSpec(
            num_scalar_prefetch=2,
            grid=(rows // SORT_TILE,),
            in_specs=[pl.BlockSpec((SORT_TILE, w), lambda j, *_: (j, 0)),
                      up(0), up(0), down(0), up(1), up(1), down(1)],
            out_specs=pl.BlockSpec((SORT_TILE, d), lambda j, *_: (j, 0))),
        out_shape=jax.ShapeDtypeStruct((rows, d), F32),
        compiler_params=_cparams("arbitrary"),
        name="moe_ffn",
    )(tcls, etab, buf, wg, wu, wd, wg, wu, wd)


def _combine_kernel(pos_ref, x_ref, mod_ref, ys_hbm, o_ref, ybuf, sem):
    tm, d = x_ref.shape
    groups = tm // SUBLANES
    i = pl.program_id(0)

    def gather(tile, op):
        slot = tile % 2

        def copy_of_row(g, j):
            p = pos_ref[tile * tm + g * SUBLANES + j]
            return pltpu.make_async_copy(_row_at(ys_hbm, p), ybuf.at[slot, g, pl.ds(j, 1)],
                                         sem.at[slot])

        _row_copies(groups, copy_of_row, op)

    @pl.when(i == 0)
    def _():
        gather(i, "start")

    @pl.when(i + 1 < pl.num_programs(0))
    def _():
        gather(i + 1, "start")

    gather(i, "wait")
    o_ref[...] = x_ref[...] + mod_ref[0, :, 5 * d:6 * d] * ybuf[i % 2].reshape(tm, d)


def _combine_call(pos, x, mod, ys, *, tm):
    t, d = x.shape
    tiles_per_mod = t // mod.shape[0] // tm
    return pl.pallas_call(
        _combine_kernel,
        grid_spec=pltpu.PrefetchScalarGridSpec(
            num_scalar_prefetch=1,
            grid=(t // tm,),
            in_specs=[pl.BlockSpec((tm, d), lambda i, *_: (i, 0)),
                      pl.BlockSpec((1, 1, mod.shape[2]), lambda i, *_: (i // tiles_per_mod, 0, 0)),
                      pl.BlockSpec(memory_space=pl.ANY)],
            out_specs=pl.BlockSpec((tm, d), lambda i, *_: (i, 0)),
            scratch_shapes=[pltpu.VMEM((2, tm // SUBLANES, SUBLANES, d), F32),
                            pltpu.SemaphoreType.DMA((2,))]),
        out_shape=jax.ShapeDtypeStruct((t, d), F32),
        compiler_params=_cparams("arbitrary"),
        name="moe_combine",
    )(pos, x, mod, _grouped(ys))


def _moe(streams, wg, wu, wd, *, tm):
    cls = jnp.concatenate([s[1] for s in streams], axis=0)
    nt = cls.shape[0]
    n_sorted_tiles = nt * tm // SORT_TILE + N_CLASSES
    pos, tcls = _rank_call(cls, n_sorted_tiles)
    pos = pos.reshape(nt * tm)
    buf = jnp.zeros((n_sorted_tiles * SORT_TILE, streams[0][0].shape[1]), F32)
    bounds = [0]
    for hx, _ in streams:
        bounds.append(bounds[-1] + hx.shape[0])
        buf = _dispatch_call(pos[bounds[-2]:bounds[-1]], hx, buf, tm=tm)
    ys = _ffn_call(tcls[0, :n_sorted_tiles], _class_experts(), buf, wg, wu, wd)
    return [pos[bounds[k]:bounds[k + 1]] for k in range(len(streams))], ys


def _rope_tables(n_tokens):
    rows = n_tokens // GRID_W
    row_ids = jnp.repeat(jnp.arange(rows), GRID_W).astype(F32)
    col_ids = jnp.tile(jnp.arange(GRID_W), rows).astype(F32)
    half = HEAD_DIM // 2
    inv = ROPE_THETA ** (-jnp.arange(0, half, 2, dtype=F32) / half)
    ang_r = row_ids[:, None] * inv
    ang_c = col_ids[:, None] * inv
    ang = jnp.concatenate([ang_r, ang_r, ang_c, ang_c] * (LANES // HEAD_DIM), axis=-1)
    return jnp.cos(ang), jnp.sin(ang)


def _head_mean_matrix():
    i = lax.broadcasted_iota(jnp.int32, (LANES, LANES), 0) // HEAD_DIM
    j = lax.broadcasted_iota(jnp.int32, (LANES, LANES), 1) // HEAD_DIM
    return jnp.where(i == j, 1.0 / HEAD_DIM, 0.0).astype(BF16)


def kernel(x, c, ctx, c_ctx, w_mod, b_mod, g_norm1, g_norm2, w_in, q_norm, k_norm, hy_short_w, hy_short_b, hy_w1, hy_b1, hy_w2, hy_b2, hy_w3, hy_freq, hy_bias, cf_dw_w, cf_dw_b, cf_ln_g, cf_ln_b, p_attn, p_hyena, p_conv, w_out, w_router, b_router, w_exp_gate, w_exp_up, w_exp_down):
    batch, seq, d = x.shape
    n_ctx = ctx.shape[1]
    depth = w_mod.shape[0]
    hy_w = hy_short_w.shape[2]
    cf_w = 2 * cf_dw_w.shape[2]
    tm = 512

    pad_rows = (-(batch + 1)) % 8
    cc = jnp.concatenate([c, c_ctx[None, :], jnp.zeros((pad_rows, d), F32)], axis=0)
    mods = _mod_call(cc, w_mod, b_mod)

    cos_x, sin_x = _rope_tables(seq)
    cos_c = jnp.ones((tm, LANES), F32)
    sin_c = jnp.zeros((tm, LANES), F32)
    hm = _head_mean_matrix()
    tile2 = lambda g: jnp.tile(g, (1, LANES // HEAD_DIM))
    qg, kg = tile2(q_norm), tile2(k_norm)

    w1p = jnp.pad(hy_w1, ((0, 0), (0, LANES - hy_w1.shape[1]), (0, 0)))
    r3 = lambda a: a[:, None, :]
    tabs_x = _dft_tables(seq)
    tabs_c = _dft_tables(n_ctx)
    filt_args = (w1p, r3(hy_b1), hy_w2, r3(hy_b2), hy_w3, r3(hy_freq))
    kr_x, ki_x, kn_x = _hyena_filter_call(seq, tabs_x, *filt_args)
    kr_c, ki_c, kn_c = _hyena_filter_call(n_ctx, tabs_c, *filt_args)

    bf = lambda a: a.astype(BF16)
    w_in_b, pa_b, ph_b, pc_b, wo_b = bf(w_in), bf(p_attn), bf(p_hyena), bf(p_conv), bf(w_out)
    wg_b, wu_b, wd_b = bf(w_exp_gate), bf(w_exp_up), bf(w_exp_down)
    wr_t = w_router.T
    br = b_router[:, None]

    xs = [x.reshape(batch * seq, d), ctx.reshape(batch * n_ctx, d)]
    pending = [None, None]
    for l in range(depth):
        last = l == depth - 1
        mods_l = [mods[l, :batch][:, None, :], mods[l, batch:batch + 1][:, None, :]]
        row = lambda a: a[l][None, :]
        consts = [(seq, cos_x, sin_x, tabs_x, kr_x, ki_x, kn_x),
                  (n_ctx, cos_c, sin_c, tabs_c, kr_c, ki_c, kn_c)]
        proj = []
        for si in range(2):
            out = _inproj_call(xs[si], mods_l[si], row(g_norm1), w_in_b[l], consts[si][1], consts[si][2],
                               qg[l:l + 1], kg[l:l + 1], hm, pending[si], tm=tm, hy_w=hy_w, cf_w=cf_w)
            if pending[si] is not None:
                xs[si] = out[-1]
            proj.append(out[:6])
        (q_x, k_x, v_x, zhy_x, zcf_x, zg_x), (q_c, k_c, v_c, zhy_c, zcf_c, zg_c) = proj
        x1s, routed = [], []
        for si in range(1 if last else 2):
            n, _, _, tabs, kr, ki, kn = consts[si]
            if si == 0:
                oa = _attn_call(q_x, [(k_x, v_x), (k_c, v_c)], batch=batch, tq=512)
                zhy, zcf, zg = zhy_x, zcf_x, zg_x
            else:
                oa = _attn_call(q_c, [(k_c, v_c)], batch=batch, tq=n_ctx)
                zhy, zcf, zg = zhy_c, zcf_c, zg_c
            oh = _hyena_call(zhy, hy_short_w[l], row(hy_short_b), tabs[0], tabs[2], kr[l], ki[l], kn[l],
                             row(hy_bias), n=n)
            oc = _conf_call(zcf, cf_dw_w[l], row(cf_dw_b), row(cf_ln_g), row(cf_ln_b), n=n)
            x1, hx, cls = _merge_call(oa, oh, oc, zg, xs[si], mods_l[si], row(g_norm2), pa_b[l], ph_b[l],
                                      pc_b[l], wo_b[l], wr_t, br, tm=tm)
            x1s.append(x1)
            routed.append((hx, cls))
        poss, ys = _moe(routed, wg_b[l], wu_b[l], wd_b[l], tm=tm)
        if last:
            return _combine_call(poss[0], x1s[0], mods_l[0], ys, tm=tm).reshape(batch, seq, d)
        for si in range(2):
            xs[si] = x1s[si]
            pending[si] = (poss[si], mods_l[si], ys)
```

```python
import functools
import math

import jax
import jax.numpy as jnp
from jax import lax
from jax.experimental import pallas as pl
from jax.experimental.pallas import tpu as pltpu

F32 = jnp.float32
BF16 = jnp.bfloat16

NORM_EPS = 1e-6
N_MOD = 6
HEAD_DIM = 64
N_Q_HEADS = 8
N_KV_HEADS = 2
ATTN_W = N_Q_HEADS * HEAD_DIM
KV_W = N_KV_HEADS * HEAD_DIM
GRID_W = 64
ROPE_THETA = 10000.0
HYENA_BANDS = 16
HYENA_FAST_DECAY = 0.3
HYENA_SLOW_DECAY = 1.5
HYENA_TARGET = 1e-2
CONF_K = 31
N_EXPERTS = 16
N_GROUPS = 4
GROUP_SIZE = N_EXPERTS // N_GROUPS

Q_SCALE = math.log2(math.e) / math.sqrt(HEAD_DIM)

LANES = 128
SUBLANES = 8
VMEM_LIMIT = 56 * 1024 * 1024


def _cparams(*sem):
    return pltpu.CompilerParams(dimension_semantics=sem, vmem_limit_bytes=VMEM_LIMIT)


def _dot(a, b):
    return jnp.dot(a, b, preferred_element_type=F32)


def _split(a):
    hi = a.astype(BF16)
    lo = (a - hi.astype(F32)).astype(BF16)
    return hi, lo


def _dot3(a, b):
    ah, al = _split(a)
    bh, bl = _split(b)
    return _dot(ah, bh) + (_dot(al, bh) + _dot(ah, bl))


def _silu(x):
    return x * jax.nn.sigmoid(x)


def _mod_kernel(c_ref, w_ref, b_ref, o_ref):
    o_ref[0] = _dot3(_silu(c_ref[...]), w_ref[0]) + b_ref[0]


def _mod_call(cc, w_mod, b_mod):
    depth, d, n = w_mod.shape
    r = cc.shape[0]
    tn = 1536
    return pl.pallas_call(
        _mod_kernel,
        grid=(depth, n // tn),
        in_specs=[pl.BlockSpec((r, d), lambda l, j: (0, 0)),
                  pl.BlockSpec((1, d, tn), lambda l, j: (l, 0, j)),
                  pl.BlockSpec((1, 1, tn), lambda l, j: (l, 0, j))],
        out_specs=pl.BlockSpec((1, r, tn), lambda l, j: (l, 0, j)),
        out_shape=jax.ShapeDtypeStruct((depth, r, n), F32),
        compiler_params=_cparams("arbitrary", "arbitrary"),
        name="adaln_mod",
    )(cc, w_mod, b_mod.reshape(depth, 1, n))


def _row_copies(n_groups, copy_of_row, op):
    def group(g, carry):
        for j in range(SUBLANES):
            cp = copy_of_row(g, j)
            if op == "start":
                cp.start(priority=j % 2)
            else:
                cp.wait()
        return carry

    lax.fori_loop(0, n_groups, group, 0)


def _grouped(a):
    return a.reshape(a.shape[0] // SUBLANES, SUBLANES, a.shape[1])


def _row_at(ref, p):
    shift = SUBLANES.bit_length() - 1
    return ref.at[lax.shift_right_logical(p, shift), pl.ds(p & (SUBLANES - 1), 1)]


def _inproj_kernel(*refs, pending):
    if pending:
        (pos_ref, x_ref, mod_ref, g_ref, w_ref, cos_ref, sin_ref, qg_ref, kg_ref, hm_ref,
         modp_ref, ys_hbm, q_ref, k_ref, v_ref, zhy_ref, zcf_ref, zg_ref, xo_ref, ybuf, sem) = refs
    else:
        (x_ref, mod_ref, g_ref, w_ref, cos_ref, sin_ref, qg_ref, kg_ref, hm_ref,
         q_ref, k_ref, v_ref, zhy_ref, zcf_ref, zg_ref) = refs
    tm, d = x_ref.shape
    x = x_ref[...]
    issue_next = lambda part, parts: None
    if pending:
        groups = tm // SUBLANES
        i = pl.program_id(0)
        last = pl.num_programs(0) - 1

        def copy_of_row(tile, slot):
            def build(g, j):
                p = pos_ref[tile * tm + g * SUBLANES + j]
                return pltpu.make_async_copy(_row_at(ys_hbm, p), ybuf.at[slot, g, pl.ds(j, 1)],
                                             sem.at[slot])
            return build

        @pl.when(i == 0)
        def _():
            _row_copies(groups, copy_of_row(i, 0), "start")

        nxt = copy_of_row(jnp.minimum(i + 1, last), (i + 1) % 2)

        def issue_next(part, parts):
            for g in range(groups * part // parts, groups * (part + 1) // parts):
                for j in range(SUBLANES):
                    nxt(g, j).start(priority=j % 2)

        _row_copies(groups, copy_of_row(i, i % 2), "wait")
        x = x + modp_ref[0, :, 5 * d:6 * d] * ybuf[i % 2].reshape(tm, d)
        xo_ref[...] = x

    ms = jnp.mean(x * x, axis=-1, keepdims=True)
    y = x * lax.rsqrt(ms + NORM_EPS) * g_ref[...]
    sh = mod_ref[0, :, 0:d]
    sc = mod_ref[0, :, d:2 * d]
    h = (y * (1.0 + sc) + sh).astype(BF16)

    cos = cos_ref[...]
    sin = sin_ref[...]
    hm = hm_ref[...]
    lane = lax.broadcasted_iota(jnp.int32, (1, LANES), 1)
    first_quarter = (lane % 32) < 16

    def norm_rope(z, gain, scale):
        hi, lo = _split(z * z)
        msq = _dot(hi, hm) + _dot(lo, hm)
        zn = z * lax.rsqrt(msq + NORM_EPS) * gain
        rot = jnp.where(first_quarter, -pltpu.roll(zn, LANES - 16, 1), pltpu.roll(zn, 16, 1))
        return (zn * cos + rot * sin) * scale

    chunk = 1024
    n_gate_chunks = zg_ref.shape[1] // chunk
    parts = 4 + n_gate_chunks
    zq = _dot(h, w_ref[:, 0:ATTN_W])
    for cb in range(ATTN_W // LANES):
        sl = slice(cb * LANES, (cb + 1) * LANES)
        q_ref[:, sl] = norm_rope(zq[:, sl], qg_ref[...], Q_SCALE).astype(BF16)
    issue_next(0, parts)
    zkv = _dot(h, w_ref[:, ATTN_W:ATTN_W + 2 * KV_W])
    k_ref[...] = norm_rope(zkv[:, 0:KV_W], kg_ref[...], 1.0).astype(BF16)
    v_ref[...] = zkv[:, KV_W:2 * KV_W].astype(BF16)
    issue_next(1, parts)
    off = ATTN_W + 2 * KV_W
    for part, ref in ((2, zhy_ref), (3, zcf_ref)):
        w = ref.shape[1]
        ref[...] = _dot(h, w_ref[:, off:off + w])
        off += w
        issue_next(part, parts)
    for cb in range(n_gate_chunks):
        zg_ref[:, cb * chunk:(cb + 1) * chunk] = _dot(
            h, w_ref[:, off + cb * chunk:off + (cb + 1) * chunk])
        issue_next(4 + cb, parts)

    if pending:
        @pl.when(i == last)
        def _():
            _row_copies(groups, nxt, "wait")


def _inproj_call(x, mod, g1, w_in, cos, sin, qg, kg, hm, pending=None, *, tm, hy_w, cf_w):
    t, d = x.shape
    tiles_per_mod = t // mod.shape[0] // tm
    n = w_in.shape[1]
    g_w = n - ATTN_W - 2 * KV_W - hy_w - cf_w
    n_pos = cos.shape[0] // tm
    row = lambda i, *_: (i, 0)
    const = lambda i, *_: (0, 0)
    mod_spec = pl.BlockSpec((1, 1, mod.shape[2]), lambda i, *_: (i // tiles_per_mod, 0, 0))
    in_specs = [pl.BlockSpec((tm, d), row), mod_spec,
                pl.BlockSpec((1, d), const),
                pl.BlockSpec((d, n), const),
                pl.BlockSpec((tm, LANES), lambda i, *_: (i % n_pos, 0)),
                pl.BlockSpec((tm, LANES), lambda i, *_: (i % n_pos, 0)),
                pl.BlockSpec((1, LANES), const),
                pl.BlockSpec((1, LANES), const),
                pl.BlockSpec((LANES, LANES), const)]
    out_specs = [pl.BlockSpec((tm, ATTN_W), row),
                 pl.BlockSpec((tm, KV_W), row),
                 pl.BlockSpec((tm, KV_W), row),
                 pl.BlockSpec((tm, hy_w), row),
                 pl.BlockSpec((tm, cf_w), row),
                 pl.BlockSpec((tm, g_w), row)]
    out_shape = [jax.ShapeDtypeStruct((t, ATTN_W), BF16),
                 jax.ShapeDtypeStruct((t, KV_W), BF16),
                 jax.ShapeDtypeStruct((t, KV_W), BF16),
                 jax.ShapeDtypeStruct((t, hy_w), F32),
                 jax.ShapeDtypeStruct((t, cf_w), F32),
                 jax.ShapeDtypeStruct((t, g_w), F32)]
    args = [x, mod, g1, w_in, cos, sin, qg, kg, hm]
    scratch = []
    if pending is not None:
        pos, mod_prev, ys = pending
        in_specs += [mod_spec, pl.BlockSpec(memory_space=pl.ANY)]
        out_specs.append(pl.BlockSpec((tm, d), row))
        out_shape.append(jax.ShapeDtypeStruct((t, d), F32))
        args = [pos] + args + [mod_prev, _grouped(ys)]
        scratch = [pltpu.VMEM((2, tm // SUBLANES, SUBLANES, d), F32), pltpu.SemaphoreType.DMA((2,))]
    return pl.pallas_call(
        functools.partial(_inproj_kernel, pending=pending is not None),
        grid_spec=pltpu.PrefetchScalarGridSpec(
            num_scalar_prefetch=0 if pending is None else 1,
            grid=(t // tm,),
            in_specs=in_specs, out_specs=out_specs, scratch_shapes=scratch),
        out_shape=out_shape,
        compiler_params=_cparams("arbitrary"),
        name="in_proj",
    )(*args)


def _attn_kernel(*refs, n_src):
    q_ref = refs[0]
    kv_refs = refs[1:1 + 2 * n_src]
    o_ref = refs[1 + 2 * n_src]
    kk_ref, vv_ref = refs[2 + 2 * n_src:]
    h = pl.program_id(1)
    lane = lax.broadcasted_iota(jnp.int32, (1, LANES), 1)
    lo_half = lane < HEAD_DIM

    @pl.when(pl.program_id(2) == 0)
    def _():
        keep = (lane // HEAD_DIM) == h
        off = 0
        for s in range(n_src):
            n = kv_refs[2 * s].shape[0]
            k = kv_refs[2 * s][...].astype(F32)
            kk_ref[off:off + n, :] = jnp.where(keep, k, pltpu.roll(k, HEAD_DIM, 1)).astype(BF16)
            v = kv_refs[2 * s + 1][...].astype(F32)
            v = jnp.where(keep, v, pltpu.roll(v, HEAD_DIM, 1))
            vv_ref[off:off + n, :] = jnp.where(lo_half, v, 1.0).astype(BF16)
            off += n

    kk = kk_ref[...]
    vv = vv_ref[...]
    n_heads = 2 * (q_ref.shape[1] // LANES)

    def scores(hd):
        q2 = q_ref[:, (hd // 2) * LANES:(hd // 2 + 1) * LANES]
        msk = lo_half if hd % 2 == 0 else jnp.logical_not(lo_half)
        qm = jnp.where(msk, q2, jnp.zeros_like(q2))
        return lax.dot_general(qm, kk, (((1,), (1,)), ((), ())), preferred_element_type=F32)

    def probs(s):
        return jnp.exp2(s - jnp.max(s, axis=-1, keepdims=True)).astype(BF16)

    def values(p):
        o = _dot(p, vv)
        return o / o[:, HEAD_DIM:HEAD_DIM + 1]

    s_all = {0: scores(0)}
    outs = {}
    p_prev = None
    for hd in range(n_heads):
        if hd + 1 < n_heads:
            s_all[hd + 1] = scores(hd + 1)
        p_cur = probs(s_all.pop(hd))
        if p_prev is not None:
            outs[hd - 1] = values(p_prev)
        p_prev = p_cur
    outs[n_heads - 1] = values(p_prev)
    for j in range(n_heads // 2):
        pair = jnp.where(lo_half, outs[2 * j], pltpu.roll(outs[2 * j + 1], HEAD_DIM, 1))
        o_ref[:, j * LANES:(j + 1) * LANES] = pair.astype(BF16)


def _attn_call(q, kvs, *, batch, tq):
    t = q.shape[0]
    nq = t // batch // tq
    gw = ATTN_W // N_KV_HEADS
    in_specs = [pl.BlockSpec((tq, gw), lambda b, h, i: (b * nq + i, h))]
    args = [q]
    nk = 0
    for k, v in kvs:
        n = k.shape[0] // batch
        nk += n
        for a in (k, v):
            in_specs.append(pl.BlockSpec((n, KV_W), lambda b, h, i: (b, 0)))
            args.append(a)
    return pl.pallas_call(
        functools.partial(_attn_kernel, n_src=len(kvs)),
        grid=(batch, N_KV_HEADS, nq),
        in_specs=in_specs,
        out_specs=pl.BlockSpec((tq, gw), lambda b, h, i: (b * nq + i, h)),
        out_shape=jax.ShapeDtypeStruct((t, ATTN_W), BF16),
        scratch_shapes=[pltpu.VMEM((nk, LANES), BF16), pltpu.VMEM((nk, LANES), BF16)],
        compiler_params=_cparams("arbitrary", "arbitrary", "arbitrary"),
        name="attention",
    )(*args)


def _hy_time_kernel(feat_ref, w1_ref, b1_ref, w2_ref, b2_ref, w3_ref, fq_ref, dec_ref,
                    sp_ref, sm_ref, kn_ref):
    n = feat_ref.shape[0]
    fq = fq_ref[0]
    h = jnp.sin(fq * (_dot3(feat_ref[...], w1_ref[0]) + b1_ref[0]))
    h = jnp.sin(fq * (_dot3(h, w2_ref[0]) + b2_ref[0]))
    h = _dot3(h, w3_ref[0]) * dec_ref[...]
    h = h / jnp.sum(jnp.abs(h), axis=0, keepdims=True)
    c = h.shape[1] // 2
    row = lax.broadcasted_iota(jnp.int32, (n, 1), 0)
    hf = h[:, 0:c]
    hb = jnp.where(row == 0, 0.0, h[:, c:2 * c])
    sp = hf + hb
    sp_ref[0] = sp
    sm_ref[0] = hf - hb
    sgn = (1 - 2 * (row & 1)).astype(F32)
    kn_ref[0] = jnp.sum(sp * sgn, axis=0, keepdims=True) * (0.5 / n)


def _hy_spec_kernel(fch_ref, fcl_ref, fsh_ref, fsl_ref, sp_ref, sm_ref, kr_ref, ki_ref, *, n):
    tk = fch_ref.shape[0]
    k = pl.program_id(1) * tk + lax.broadcasted_iota(jnp.int32, (tk, 1), 0)
    scl = jnp.where(k == 0, 0.5 / n, 1.0 / n)
    sph, spl = _split(sp_ref[0])
    smh, sml = _split(sm_ref[0])
    kr = _dot(fch_ref[...], sph) + (_dot(fcl_ref[...], sph) + _dot(fch_ref[...], spl))
    ki = _dot(fsh_ref[...], smh) + (_dot(fsl_ref[...], smh) + _dot(fsh_ref[...], sml))
    kr_ref[0] = kr * scl
    ki_ref[0] = ki * scl


def _dft_tables(n):
    k = lax.broadcasted_iota(jnp.int32, (n, n), 0)
    t = lax.broadcasted_iota(jnp.int32, (n, n), 1)
    ang = ((k * t) % (2 * n)).astype(F32) * (math.pi / n)
    return _split(jnp.cos(ang)) + _split(-jnp.sin(ang))


def _hyena_features(n, c):
    t = jnp.linspace(0.0, 1.0, n, dtype=F32)[:, None]
    w = 2.0 * math.pi * jnp.arange(n, dtype=F32)[:, None] / n
    f = jnp.linspace(1e-4, HYENA_BANDS - 1, HYENA_BANDS, dtype=F32)[None, :]
    feat = jnp.concatenate([t, jnp.cos(f * w), -jnp.sin(f * w)], axis=-1)
    feat = jnp.pad(feat, ((0, 0), (0, LANES - feat.shape[1])))
    max_decay = math.log(HYENA_TARGET) / HYENA_FAST_DECAY
    min_decay = math.log(HYENA_TARGET) / HYENA_SLOW_DECAY
    deltas = jnp.tile(jnp.linspace(min_decay, max_decay, c, dtype=F32), 2)
    return feat, jnp.exp(-t * jnp.abs(deltas))


def _hyena_filter_call(n, tabs, w1p, b1, w2, b2, w3, fq):
    depth = w1p.shape[0]
    ffn = w2.shape[1]
    c = w3.shape[2] // 2
    feat, dec = _hyena_features(n, c)
    lay = lambda l: (l, 0, 0)
    const = lambda l: (0, 0)
    sp, sm, kn = pl.pallas_call(
        _hy_time_kernel,
        grid=(depth,),
        in_specs=[pl.BlockSpec((n, LANES), const),
                  pl.BlockSpec((1, LANES, ffn), lay), pl.BlockSpec((1, 1, ffn), lay),
                  pl.BlockSpec((1, ffn, ffn), lay), pl.BlockSpec((1, 1, ffn), lay),
                  pl.BlockSpec((1, ffn, 2 * c), lay), pl.BlockSpec((1, 1, ffn), lay),
                  pl.BlockSpec((n, 2 * c), const)],
        out_specs=[pl.BlockSpec((1, n, c), lay), pl.BlockSpec((1, n, c), lay),
                   pl.BlockSpec((1, 1, c), lay)],
        out_shape=[jax.ShapeDtypeStruct((depth, n, c), F32), jax.ShapeDtypeStruct((depth, n, c), F32),
                   jax.ShapeDtypeStruct((depth, 1, c), F32)],
        compiler_params=_cparams("arbitrary"),
        name="hyena_filter_time",
    )(feat, w1p, b1, w2, b2, w3, fq, dec)
    tk = min(n, 512)
    mat = pl.BlockSpec((tk, n), lambda l, j: (j, 0))
    vec = pl.BlockSpec((1, n, c), lambda l, j: (l, 0, 0))
    out = pl.BlockSpec((1, tk, c), lambda l, j: (l, j, 0))
    kr, ki = pl.pallas_call(
        functools.partial(_hy_spec_kernel, n=n),
        grid=(depth, n // tk),
        in_specs=[mat, mat, mat, mat, vec, vec],
        out_specs=[out, out],
        out_shape=[jax.ShapeDtypeStruct((depth, n, c), F32)] * 2,
        compiler_params=_cparams("arbitrary", "arbitrary"),
        name="hyena_filter_spectrum",
    )(*tabs, sp, sm)
    return kr, ki, kn


HYENA_ROW_CHUNK = 512


def _hyena_kernel(z_ref, sw_ref, sb_ref, fc_ref, fs_ref, kr_ref, ki_ref, kn_ref, db_ref, o_ref):
    n = z_ref.shape[0]
    c = o_ref.shape[1]
    z = z_ref[...]
    row = lax.broadcasted_iota(jnp.int32, (n, 1), 0)
    z_prev = jnp.where(row == 0, 0.0, pltpu.roll(z, 1, 0))
    z_next = jnp.where(row == n - 1, 0.0, pltpu.roll(z, n - 1, 0))
    zc = z_prev * sw_ref[0:1, :] + z * sw_ref[1:2, :] + z_next * sw_ref[2:3, :] + sb_ref[...]
    x0 = zc[:, 0:c]
    u = zc[:, 2 * c:3 * c] * zc[:, c:2 * c]
    ub = u.astype(BF16)
    sgn = (1 - 2 * (row & 1)).astype(F32)
    nyq = jnp.sum(u * sgn, axis=0, keepdims=True) * kn_ref[...]
    chunk = min(n, HYENA_ROW_CHUNK)
    chunks = [slice(r, r + chunk) for r in range(0, n, chunk)]
    pr, pi = [], []
    for rs in chunks:
        ur = _dot(fc_ref[rs, :], ub)
        ui = _dot(fs_ref[rs, :], ub)
        kr = kr_ref[rs, :]
        ki = ki_ref[rs, :]
        pr.append((ur * kr - ui * ki).astype(BF16))
        pi.append((ur * ki + ui * kr).astype(BF16))
    pr = jnp.concatenate(pr, axis=0)
    pi = jnp.concatenate(pi, axis=0)
    for rs in chunks:
        y = _dot(fc_ref[rs, :], pr) + _dot(fs_ref[rs, :], pi) + sgn[rs, :] * nyq
        o_ref[rs, :] = ((y + u[rs, :] * db_ref[...]) * x0[rs, :]).astype(BF16)


def _hyena_call(z, sw, sb, fc, fs, kr, ki, kn, db, *, n):
    t, w = z.shape
    c = w // 3
    const = lambda b: (0, 0)
    return pl.pallas_call(
        _hyena_kernel,
        grid=(t // n,),
        in_specs=[pl.BlockSpec((n, w), lambda b: (b, 0)),
                  pl.BlockSpec((3, w), const), pl.BlockSpec((1, w), const),
                  pl.BlockSpec((n, n), const), pl.BlockSpec((n, n), const),
                  pl.BlockSpec((n, c), const), pl.BlockSpec((n, c), const),
                  pl.BlockSpec((1, c), const), pl.BlockSpec((1, c), const)],
        out_specs=pl.BlockSpec((n, c), lambda b: (b, 0)),
        out_shape=jax.ShapeDtypeStruct((t, c), BF16),
        compiler_params=_cparams("arbitrary"),
        name="hyena_conv",
    )(z, sw, sb, fc, fs, kr, ki, kn, db)


CONF_PAD = 16


def _conf_kernel(z_ref, w_ref, b_ref, g_ref, beta_ref, o_ref, pad_ref):
    n = z_ref.shape[0]
    c = o_ref.shape[1]
    u = z_ref[:, 0:c] * jax.nn.sigmoid(z_ref[:, c:2 * c])
    zeros = jnp.zeros((CONF_PAD, c), F32)
    pad_ref[0:CONF_PAD, :] = zeros
    pad_ref[CONF_PAD + n:2 * CONF_PAD + n, :] = zeros
    pad_ref[CONF_PAD:CONF_PAD + n, :] = u
    first = CONF_PAD - (CONF_K - 1) // 2
    acc = b_ref[...]
    for r in range(SUBLANES):
        part = None
        for off in range(r, first + CONF_K, SUBLANES):
            j = off - first
            if j < 0:
                continue
            term = w_ref[j:j + 1, :] * pad_ref[off - r:off - r + n + SUBLANES, :]
            part = term if part is None else part + term
        acc = acc + part[r:r + n, :]
    mu = jnp.mean(acc, axis=-1, keepdims=True)
    dlt = acc - mu
    var = jnp.mean(dlt * dlt, axis=-1, keepdims=True)
    y = dlt * lax.rsqrt(var + NORM_EPS) * g_ref[...] + beta_ref[...]
    o_ref[...] = _silu(y).astype(BF16)


def _conf_call(z, w, b, g, beta, *, n):
    t, w2 = z.shape
    c = w2 // 2
    const = lambda i: (0, 0)
    return pl.pallas_call(
        _conf_kernel,
        grid=(t // n,),
        in_specs=[pl.BlockSpec((n, w2), lambda i: (i, 0)),
                  pl.BlockSpec((CONF_K, c), const), pl.BlockSpec((1, c), const),
                  pl.BlockSpec((1, c), const), pl.BlockSpec((1, c), const)],
        out_specs=pl.BlockSpec((n, c), lambda i: (i, 0)),
        out_shape=jax.ShapeDtypeStruct((t, c), BF16),
        scratch_shapes=[pltpu.VMEM((n + 2 * CONF_PAD, c), F32)],
        compiler_params=_cparams("arbitrary"),
        name="conformer_conv",
    )(z, w, b, g, beta)


def _route(sel, s):
    grp_score = []
    for g in range(N_GROUPS):
        r = sel[g * GROUP_SIZE:(g + 1) * GROUP_SIZE]
        best = None
        for i in range(GROUP_SIZE):
            for j in range(i + 1, GROUP_SIZE):
                pair = r[i] + r[j]
                best = pair if best is None else jnp.maximum(best, pair)
        grp_score.append(best)
    gbest = grp_score[0]
    gidx = jnp.zeros_like(gbest, dtype=jnp.int32)
    for g in range(1, N_GROUPS):
        better = grp_score[g] > gbest
        gidx = jnp.where(better, g, gidx)
        gbest = jnp.where(better, grp_score[g], gbest)
    neg = jnp.full_like(gbest, -jnp.inf)
    masked = [jnp.where(gidx == e // GROUP_SIZE, sel[e], neg) for e in range(N_EXPERTS)]

    def argmax_first(vals):
        best = vals[0]
        idx = jnp.zeros_like(gidx)
        for e in range(1, N_EXPERTS):
            better = vals[e] > best
            idx = jnp.where(better, e, idx)
            best = jnp.where(better, vals[e], best)
        return idx

    i1 = argmax_first(masked)
    i2 = argmax_first([jnp.where(i1 == e, neg, masked[e]) for e in range(N_EXPERTS)])
    zero = jnp.zeros_like(gbest)
    w1 = sum(jnp.where(i1 == e, s[e], zero) for e in range(N_EXPERTS))
    w2 = sum(jnp.where(i2 == e, s[e], zero) for e in range(N_EXPERTS))
    tot = w1 + w2
    w1, w2 = w1 / tot, w2 / tot
    swap = i2 < i1
    a = jnp.where(swap, i2, i1) - GROUP_SIZE * gidx
    b = jnp.where(swap, i1, i2) - GROUP_SIZE * gidx
    cls = PAIRS_PER_GROUP * gidx + ((a * (2 * GROUP_SIZE - 1 - a)) >> 1) + (b - a - 1)
    return cls, jnp.where(swap, w2, w1), jnp.where(swap, w1, w2)


PAIRS_PER_GROUP = GROUP_SIZE * (GROUP_SIZE - 1) // 2
N_CLASSES = N_GROUPS * PAIRS_PER_GROUP
SORT_TILE = 512


def _class_experts():
    lo, hi = [], []
    for g in range(N_GROUPS):
        for a in range(GROUP_SIZE):
            for b in range(a + 1, GROUP_SIZE):
                lo.append(g * GROUP_SIZE + a)
                hi.append(g * GROUP_SIZE + b)
    return jnp.asarray(lo + hi, dtype=jnp.int32)


def _col(row):
    return jnp.broadcast_to(row, (LANES, row.shape[1])).T[:, 0:1]


def _row(col):
    return jnp.broadcast_to(col, (col.shape[0], LANES)).T[0:1, :]


STATE_ROWS = 8
STATE_LANES = 256


def _initial_state():
    lane = lax.broadcasted_iota(jnp.int32, (1, STATE_LANES), 1)
    zero = jnp.zeros((1, STATE_LANES), jnp.int32)
    rows = [zero, jnp.where(lane < N_CLASSES, lane, 0), zero + N_CLASSES,
            jnp.where(lane < N_CLASSES, lane, LANES)] + [zero] * (STATE_ROWS - 4)
    return jnp.concatenate(rows, axis=0).astype(F32)


def _allocate_rows(cls, st_ref):
    tm = cls.shape[1]
    cls_col = _col(cls.astype(F32)).astype(jnp.int32)
    lane = lax.broadcasted_iota(jnp.int32, (1, LANES), 1)
    onehot = (cls_col == lane).astype(F32)
    counts = jnp.sum(onehot, axis=0, keepdims=True)
    t = lax.broadcasted_iota(jnp.int32, (tm, tm), 0)
    s = lax.broadcasted_iota(jnp.int32, (tm, tm), 1)
    earlier = _dot((s < t).astype(BF16), onehot.astype(BF16))
    fill = st_ref[0:1, 0:LANES]
    cur = st_ref[1:2, 0:LANES]
    nfree = st_ref[2:3, 0:LANES]
    tot = fill + counts
    full = tot >= SORT_TILE
    full_f = full.astype(F32)
    r = lax.broadcasted_iota(jnp.int32, (LANES, LANES), 0)
    c = lax.broadcasted_iota(jnp.int32, (LANES, LANES), 1)
    before = _dot(jnp.broadcast_to(full_f, (SUBLANES, LANES)).astype(BF16), (r < c).astype(BF16))[0:1, :]
    newid = nfree + before
    rank = jnp.sum(onehot * (fill + earlier), axis=1, keepdims=True)
    cur_base = jnp.sum(onehot * cur, axis=1, keepdims=True) * SORT_TILE
    new_base = jnp.sum(onehot * newid, axis=1, keepdims=True) * SORT_TILE
    pos_col = jnp.where(rank < SORT_TILE, cur_base + rank, new_base + (rank - SORT_TILE))
    st_ref[0:1, 0:LANES] = jnp.where(full, tot - SORT_TILE, tot)
    st_ref[1:2, 0:LANES] = jnp.where(full, newid, cur)
    st_ref[2:3, :] = st_ref[2:3, :] + jnp.sum(full_f, axis=1, keepdims=True)
    chunk = lax.broadcasted_iota(jnp.int32, (1, STATE_LANES), 1).astype(F32)
    opened = jnp.logical_and(_col(newid) == chunk, _col(full_f) > 0.0)
    cls_sub = lax.broadcasted_iota(jnp.int32, (LANES, 1), 0).astype(F32)
    chunk_cls = jnp.sum(jnp.where(opened, cls_sub, 0.0), axis=0, keepdims=True)
    hit = jnp.sum(opened.astype(F32), axis=0, keepdims=True)
    st_ref[3:4, :] = jnp.where(hit > 0.0, chunk_cls, st_ref[3:4, :])
    return _row(pos_col).astype(jnp.int32)


def _merge_kernel(oa_ref, oh_ref, oc_ref, zg_ref, x_ref, mod_ref, g2_ref,
                  pa_ref, ph_ref, pc_ref, wo_ref, wr_ref, br_ref, state_ref, buf_in,
                  xo_ref, pos_ref, state_out_ref, buf_out,
                  hxs, posv, poss, st, row_sem, pos_sem):
    del buf_in
    tm, d = x_ref.shape
    groups = tm // SUBLANES
    i = pl.program_id(0)
    last = pl.num_programs(0) - 1
    slot = i % 2
    other = (i + 1) % 2
    dump_row = buf_out.shape[0] * SUBLANES - tm

    def row_copy(sl):
        def build(g, j):
            return pltpu.make_async_copy(hxs.at[sl, g, pl.ds(j, 1)],
                                         _row_at(buf_out, poss[sl, g * SUBLANES + j]), row_sem.at[sl])
        return build

    def pos_copy(sl):
        return pltpu.make_async_copy(posv.at[0], poss.at[sl], pos_sem)

    @pl.when(i == 0)
    def _():
        st[...] = state_ref[...]
        hxs[1] = jnp.zeros(hxs.shape[1:], F32)

        def fill_dump(r, carry):
            poss[1, r] = dump_row + r
            return carry

        lax.fori_loop(0, tm, fill_dump, 0)

    @pl.when(i >= 1)
    def _():
        pos_copy(other).wait()
        _row_copies(groups, row_copy(slot), "wait")

    send_prev = row_copy(other)

    def issue_prev(part, parts):
        for g in range(groups * part // parts, groups * (part + 1) // parts):
            for j in range(SUBLANES):
                send_prev(g, j).start(priority=j % 2)

    merged = jax.nn.sigmoid(zg_ref[:, 0:d]) * _dot(oa_ref[...], pa_ref[...])
    issue_prev(0, 4)
    merged += jax.nn.sigmoid(zg_ref[:, d:2 * d]) * _dot(oh_ref[...], ph_ref[...])
    issue_prev(1, 4)
    merged += jax.nn.sigmoid(zg_ref[:, 2 * d:3 * d]) * _dot(oc_ref[...], pc_ref[...])
    issue_prev(2, 4)
    y = _dot(merged.astype(BF16), wo_ref[...])
    issue_prev(3, 4)
    x = x_ref[...] + mod_ref[0, :, 2 * d:3 * d] * y
    xo_ref[...] = x
    ms = jnp.mean(x * x, axis=-1, keepdims=True)
    h2 = x * lax.rsqrt(ms + NORM_EPS) * g2_ref[...]
    h2 = h2 * (1.0 + mod_ref[0, :, 4 * d:5 * d]) + mod_ref[0, :, 3 * d:4 * d]
    hxs[slot, :, :, 0:d] = h2.reshape(groups, SUBLANES, d)
    hh, hl = _split(h2)
    wh, wl = _split(wr_ref[...])
    nt = (((1,), (1,)), ((), ()))
    logits = (lax.dot_general(wh, hh, nt, preferred_element_type=F32)
              + (lax.dot_general(wh, hl, nt, preferred_element_type=F32)
                 + lax.dot_general(wl, hh, nt, preferred_element_type=F32)))
    s = jax.nn.sigmoid(logits)
    sel = s + br_ref[...]
    cls, w_lo, w_hi = _route([sel[e:e + 1, :] for e in range(N_EXPERTS)],
                             [s[e:e + 1, :] for e in range(N_EXPERTS)])
    sub = lax.broadcasted_iota(jnp.int32, (LANES, tm), 0)
    rows = jnp.where(sub == 0, w_lo, jnp.where(sub == 1, w_hi, 0.0))
    hxs[slot, :, :, d:d + LANES] = rows.T.reshape(groups, SUBLANES, LANES)
    pos = _allocate_rows(cls, st)
    pos_ref[0] = pos
    posv[...] = pos
    pos_copy(slot).start()

    @pl.when(i == last)
    def _():
        pos_copy(slot).wait()
        _row_copies(groups, row_copy(slot), "start")
        _row_copies(groups, send_prev, "wait")
        _row_copies(groups, row_copy(slot), "wait")
        state_out_ref[...] = st[...]


def _merge_call(oa, oh, oc, zg, x, mod, g2, pa, ph, pc, wo, wr_t, br, state, buf, *, tm):
    t, d = x.shape
    w = d + LANES
    tiles_per_mod = t // mod.shape[0] // tm
    groups = tm // SUBLANES
    row = lambda i: (i, 0)
    const = lambda i: (0, 0)
    full = lambda a: pl.BlockSpec(a.shape, const)
    any_spec = pl.BlockSpec(memory_space=pl.ANY)
    x1, pos, state, buf = pl.pallas_call(
        _merge_kernel,
        grid=(t // tm,),
        in_specs=[pl.BlockSpec((tm, oa.shape[1]), row), pl.BlockSpec((tm, oh.shape[1]), row),
                  pl.BlockSpec((tm, oc.shape[1]), row), pl.BlockSpec((tm, zg.shape[1]), row),
                  pl.BlockSpec((tm, d), row),
                  pl.BlockSpec((1, 1, mod.shape[2]), lambda i: (i // tiles_per_mod, 0, 0)),
                  full(g2), full(pa), full(ph), full(pc), full(wo), full(wr_t), full(br),
                  full(state), any_spec],
        out_specs=[pl.BlockSpec((tm, d), row), pl.BlockSpec((1, 1, tm), lambda i: (i, 0, 0)),
                   full(state), any_spec],
        out_shape=[jax.ShapeDtypeStruct((t, d), F32), jax.ShapeDtypeStruct((t // tm, 1, tm), jnp.int32),
                   jax.ShapeDtypeStruct(state.shape, F32),
                   jax.ShapeDtypeStruct((buf.shape[0] // SUBLANES, SUBLANES, w), F32)],
        input_output_aliases={14: 3},
        scratch_shapes=[pltpu.VMEM((2, groups, SUBLANES, w), F32), pltpu.VMEM((1, tm), jnp.int32),
                        pltpu.SMEM((2, tm), jnp.int32), pltpu.VMEM(state.shape, F32),
                        pltpu.SemaphoreType.DMA((2,)), pltpu.SemaphoreType.DMA],
        compiler_params=_cparams("arbitrary"),
        name="merge_router",
    )(oa, oh, oc, zg, x, mod, g2, pa, ph, pc, wo, wr_t, br, state, _grouped(buf))
    return x1, pos.reshape(t), state, buf.reshape(-1, w)


def _ffn_kernel(tcls_ref, etab_ref, xs_ref, wga_ref, wua_ref, wda_ref, wgb_ref, wub_ref, wdb_ref, o_ref):
    del etab_ref
    d = o_ref.shape[1]
    valid = tcls_ref[pl.program_id(0)] < N_CLASSES

    @pl.when(valid)
    def _():
        x = xs_ref[:, 0:d].astype(BF16)
        ha = _silu(_dot(x, wga_ref[0])) * _dot(x, wua_ref[0]) * xs_ref[:, d:d + 1]
        hb = _silu(_dot(x, wgb_ref[0])) * _dot(x, wub_ref[0]) * xs_ref[:, d + 1:d + 2]
        o_ref[...] = _dot(ha.astype(BF16), wda_ref[0]) + _dot(hb.astype(BF16), wdb_ref[0])

    @pl.when(jnp.logical_not(valid))
    def _():
        o_ref[...] = jnp.zeros_like(o_ref)


def _ffn_call(tcls, etab, buf, wg, wu, wd):
    rows, w = tcls.shape[0] * SORT_TILE, buf.shape[1]
    ne, d, ff = wg.shape

    def expert(which):
        def index(j, tcls_ref, etab_ref):
            c = jnp.minimum(tcls_ref[j], N_CLASSES - 1)
            return (etab_ref[which * N_CLASSES + c], 0, 0)
        return index

    up = lambda which: pl.BlockSpec((1, d, ff), expert(which))
    down = lambda which: pl.BlockSpec((1, ff, d), expert(which))
    return pl.pallas_call(
        _ffn_kernel,
        grid_spec=pltpu.PrefetchScalarGridSpec(
            num_scalar_prefetch=2,
            grid=(rows // SORT_TILE,),
            in_specs=[pl.BlockSpec((SORT_TILE, w), lambda j, *_: (j, 0)),
                      up(0), up(0), down(0), up(1), up(1), down(1)],
            out_specs=pl.BlockSpec((SORT_TILE, d), lambda j, *_: (j, 0))),
        out_shape=jax.ShapeDtypeStruct((rows, d), F32),
        compiler_params=_cparams("arbitrary"),
        name="moe_ffn",
    )(tcls, etab, buf, wg, wu, wd, wg, wu, wd)


def _combine_kernel(pos_ref, x_ref, mod_ref, ys_hbm, o_ref, ybuf, sem):
    tm, d = x_ref.shape
    groups = tm // SUBLANES
    i = pl.program_id(0)

    def gather(tile, op):
        slot = tile % 2

        def copy_of_row(g, j):
            p = pos_ref[tile * tm + g * SUBLANES + j]
            return pltpu.make_async_copy(_row_at(ys_hbm, p), ybuf.at[slot, g, pl.ds(j, 1)],
                                         sem.at[slot])

        _row_copies(groups, copy_of_row, op)

    @pl.when(i == 0)
    def _():
        gather(i, "start")

    @pl.when(i + 1 < pl.num_programs(0))
    def _():
        gather(i + 1, "start")

    gather(i, "wait")
    o_ref[...] = x_ref[...] + mod_ref[0, :, 5 * d:6 * d] * ybuf[i % 2].reshape(tm, d)


def _combine_call(pos, x, mod, ys, *, tm):
    t, d = x.shape
    tiles_per_mod = t // mod.shape[0] // tm
    return pl.pallas_call(
        _combine_kernel,
        grid_spec=pltpu.PrefetchScalarGridSpec(
            num_scalar_prefetch=1,
            grid=(t // tm,),
            in_specs=[pl.BlockSpec((tm, d), lambda i, *_: (i, 0)),
                      pl.BlockSpec((1, 1, mod.shape[2]), lambda i, *_: (i // tiles_per_mod, 0, 0)),
                      pl.BlockSpec(memory_space=pl.ANY)],
            out_specs=pl.BlockSpec((tm, d), lambda i, *_: (i, 0)),
            scratch_shapes=[pltpu.VMEM((2, tm // SUBLANES, SUBLANES, d), F32),
                            pltpu.SemaphoreType.DMA((2,))]),
        out_shape=jax.ShapeDtypeStruct((t, d), F32),
        compiler_params=_cparams("arbitrary"),
        name="moe_combine",
    )(pos, x, mod, _grouped(ys))


def _rope_tables(n_tokens):
    rows = n_tokens // GRID_W
    row_ids = jnp.repeat(jnp.arange(rows), GRID_W).astype(F32)
    col_ids = jnp.tile(jnp.arange(GRID_W), rows).astype(F32)
    half = HEAD_DIM // 2
    inv = ROPE_THETA ** (-jnp.arange(0, half, 2, dtype=F32) / half)
    ang_r = row_ids[:, None] * inv
    ang_c = col_ids[:, None] * inv
    ang = jnp.concatenate([ang_r, ang_r, ang_c, ang_c] * (LANES // HEAD_DIM), axis=-1)
    return jnp.cos(ang), jnp.sin(ang)


def _head_mean_matrix():
    i = lax.broadcasted_iota(jnp.int32, (LANES, LANES), 0) // HEAD_DIM
    j = lax.broadcasted_iota(jnp.int32, (LANES, LANES), 1) // HEAD_DIM
    return jnp.where(i == j, 1.0 / HEAD_DIM, 0.0).astype(BF16)


def kernel(x, c, ctx, c_ctx, w_mod, b_mod, g_norm1, g_norm2, w_in, q_norm, k_norm, hy_short_w, hy_short_b, hy_w1, hy_b1, hy_w2, hy_b2, hy_w3, hy_freq, hy_bias, cf_dw_w, cf_dw_b, cf_ln_g, cf_ln_b, p_attn, p_hyena, p_conv, w_out, w_router, b_router, w_exp_gate, w_exp_up, w_exp_down):
    batch, seq, d = x.shape
    n_ctx = ctx.shape[1]
    depth = w_mod.shape[0]
    hy_w = hy_short_w.shape[2]
    cf_w = 2 * cf_dw_w.shape[2]
    tm = 512

    pad_rows = (-(batch + 1)) % 8
    cc = jnp.concatenate([c, c_ctx[None, :], jnp.zeros((pad_rows, d), F32)], axis=0)
    mods = _mod_call(cc, w_mod, b_mod)

    cos_x, sin_x = _rope_tables(seq)
    cos_c = jnp.ones((tm, LANES), F32)
    sin_c = jnp.zeros((tm, LANES), F32)
    hm = _head_mean_matrix()
    tile2 = lambda g: jnp.tile(g, (1, LANES // HEAD_DIM))
    qg, kg = tile2(q_norm), tile2(k_norm)

    w1p = jnp.pad(hy_w1, ((0, 0), (0, LANES - hy_w1.shape[1]), (0, 0)))
    r3 = lambda a: a[:, None, :]
    tabs_x = _dft_tables(seq)
    tabs_c = _dft_tables(n_ctx)
    filt_args = (w1p, r3(hy_b1), hy_w2, r3(hy_b2), hy_w3, r3(hy_freq))
    kr_x, ki_x, kn_x = _hyena_filter_call(seq, tabs_x, *filt_args)
    kr_c, ki_c, kn_c = _hyena_filter_call(n_ctx, tabs_c, *filt_args)

    bf = lambda a: a.astype(BF16)
    w_in_b, pa_b, ph_b, pc_b, wo_b = bf(w_in), bf(p_attn), bf(p_hyena), bf(p_conv), bf(w_out)
    wg_b, wu_b, wd_b = bf(w_exp_gate), bf(w_exp_up), bf(w_exp_down)
    wr_t = w_router.T
    br = b_router[:, None]

    xs = [x.reshape(batch * seq, d), ctx.reshape(batch * n_ctx, d)]
    pending = [None, None]
    for l in range(depth):
        last = l == depth - 1
        mods_l = [mods[l, :batch][:, None, :], mods[l, batch:batch + 1][:, None, :]]
        row = lambda a: a[l][None, :]
        consts = [(seq, cos_x, sin_x, tabs_x, kr_x, ki_x, kn_x),
                  (n_ctx, cos_c, sin_c, tabs_c, kr_c, ki_c, kn_c)]
        proj = []
        for si in range(2):
            out = _inproj_call(xs[si], mods_l[si], row(g_norm1), w_in_b[l], consts[si][1], consts[si][2],
                               qg[l:l + 1], kg[l:l + 1], hm, pending[si], tm=tm, hy_w=hy_w, cf_w=cf_w)
            if pending[si] is not None:
                xs[si] = out[-1]
            proj.append(out[:6])
        (q_x, k_x, v_x, zhy_x, zcf_x, zg_x), (q_c, k_c, v_c, zhy_c, zcf_c, zg_c) = proj
        n_streams = 1 if last else 2
        n_sorted_tiles = sum(a.shape[0] for a in xs[:n_streams]) // SORT_TILE + N_CLASSES
        assert n_sorted_tiles <= STATE_LANES
        state = _initial_state()
        buf = jnp.zeros(((n_sorted_tiles + 1) * SORT_TILE, d + LANES), F32)
        x1s, poss = [], []
        for si in range(n_streams):
            n, _, _, tabs, kr, ki, kn = consts[si]
            if si == 0:
                oa = _attn_call(q_x, [(k_x, v_x), (k_c, v_c)], batch=batch, tq=512)
                zhy, zcf, zg = zhy_x, zcf_x, zg_x
            else:
                oa = _attn_call(q_c, [(k_c, v_c)], batch=batch, tq=n_ctx)
                zhy, zcf, zg = zhy_c, zcf_c, zg_c
            oh = _hyena_call(zhy, hy_short_w[l], row(hy_short_b), tabs[0], tabs[2], kr[l], ki[l], kn[l],
                             row(hy_bias), n=n)
            oc = _conf_call(zcf, cf_dw_w[l], row(cf_dw_b), row(cf_ln_g), row(cf_ln_b), n=n)
            x1, pos, state, buf = _merge_call(oa, oh, oc, zg, xs[si], mods_l[si], row(g_norm2), pa_b[l],
                                              ph_b[l], pc_b[l], wo_b[l], wr_t, br, state, buf, tm=tm)
            x1s.append(x1)
            poss.append(pos)
        tcls = state[3, :n_sorted_tiles].astype(jnp.int32)
        ys = _ffn_call(tcls, _class_experts(), buf, wg_b[l], wu_b[l], wd_b[l])
        if last:
            return _combine_call(poss[0], x1s[0], mods_l[0], ys, tm=tm).reshape(batch, seq, d)
        for si in range(2):
            xs[si] = x1s[si]
            pending[si] = (poss[si], mods_l[si], ys)
```

```python
import functools
import math

import jax
import jax.numpy as jnp
from jax import lax
from jax.experimental import pallas as pl
from jax.experimental.pallas import tpu as pltpu

F32 = jnp.float32
BF16 = jnp.bfloat16

NORM_EPS = 1e-6
N_MOD = 6
HEAD_DIM = 64
N_Q_HEADS = 8
N_KV_HEADS = 2
ATTN_W = N_Q_HEADS * HEAD_DIM
KV_W = N_KV_HEADS * HEAD_DIM
GRID_W = 64
ROPE_THETA = 10000.0
HYENA_BANDS = 16
HYENA_FAST_DECAY = 0.3
HYENA_SLOW_DECAY = 1.5
HYENA_TARGET = 1e-2
CONF_K = 31
N_EXPERTS = 16
N_GROUPS = 4
GROUP_SIZE = N_EXPERTS // N_GROUPS

Q_SCALE = math.log2(math.e) / math.sqrt(HEAD_DIM)

LANES = 128
SUBLANES = 8
VMEM_LIMIT = 56 * 1024 * 1024


def _cparams(*sem):
    return pltpu.CompilerParams(dimension_semantics=sem, vmem_limit_bytes=VMEM_LIMIT)


def _dot(a, b):
    return jnp.dot(a, b, preferred_element_type=F32)


def _split(a):
    hi = a.astype(BF16)
    lo = (a - hi.astype(F32)).astype(BF16)
    return hi, lo


def _dot3(a, b):
    ah, al = _split(a)
    bh, bl = _split(b)
    return _dot(ah, bh) + (_dot(al, bh) + _dot(ah, bl))


def _silu(x):
    return x * jax.nn.sigmoid(x)


def _mod_kernel(c_ref, w_ref, b_ref, o_ref):
    o_ref[0] = _dot3(_silu(c_ref[...]), w_ref[0]) + b_ref[0]


def _mod_call(cc, w_mod, b_mod):
    depth, d, n = w_mod.shape
    r = cc.shape[0]
    tn = 1536
    return pl.pallas_call(
        _mod_kernel,
        grid=(depth, n // tn),
        in_specs=[pl.BlockSpec((r, d), lambda l, j: (0, 0)),
                  pl.BlockSpec((1, d, tn), lambda l, j: (l, 0, j)),
                  pl.BlockSpec((1, 1, tn), lambda l, j: (l, 0, j))],
        out_specs=pl.BlockSpec((1, r, tn), lambda l, j: (l, 0, j)),
        out_shape=jax.ShapeDtypeStruct((depth, r, n), F32),
        compiler_params=_cparams("arbitrary", "arbitrary"),
        name="adaln_mod",
    )(cc, w_mod, b_mod.reshape(depth, 1, n))


def _row_copies(n_groups, copy_of_row, op):
    def group(g, carry):
        for j in range(SUBLANES):
            cp = copy_of_row(g, j)
            if op == "start":
                cp.start(priority=j % 2)
            else:
                cp.wait()
        return carry

    lax.fori_loop(0, n_groups, group, 0)


def _grouped(a):
    return a.reshape(a.shape[0] // SUBLANES, SUBLANES, a.shape[1])


def _row_at(ref, p):
    shift = SUBLANES.bit_length() - 1
    return ref.at[lax.shift_right_logical(p, shift), pl.ds(p & (SUBLANES - 1), 1)]


def _inproj_kernel(*refs, pending, n_steps):
    if pending:
        (pos_ref, x_ref, mod_ref, g_ref, w_ref, cos_ref, sin_ref, qg_ref, kg_ref, hm_ref,
         modp_ref, ys_hbm, q_ref, k_ref, v_ref, zhy_ref, zcf_ref, zg_ref, xo_ref, ybuf, sem) = refs
    else:
        (x_ref, mod_ref, g_ref, w_ref, cos_ref, sin_ref, qg_ref, kg_ref, hm_ref,
         q_ref, k_ref, v_ref, zhy_ref, zcf_ref, zg_ref) = refs
    tm, d = x_ref.shape
    x = x_ref[...]
    issue_next = lambda part, parts: None
    if pending:
        groups = tm // SUBLANES
        i = pl.program_id(0)
        last = n_steps - 1

        def copy_of_row(tile, slot):
            def build(g, j):
                p = pos_ref[tile * tm + g * SUBLANES + j]
                return pltpu.make_async_copy(_row_at(ys_hbm, p), ybuf.at[slot, g, pl.ds(j, 1)],
                                             sem.at[slot])
            return build

        @pl.when(i == 0)
        def _():
            _row_copies(groups, copy_of_row(i, 0), "start")

        nxt = copy_of_row(jnp.minimum(i + 1, last), (i + 1) % 2)

        def issue_next(part, parts):
            for g in range(groups * part // parts, groups * (part + 1) // parts):
                for j in range(SUBLANES):
                    nxt(g, j).start(priority=j % 2)

        _row_copies(groups, copy_of_row(i, i % 2), "wait")
        x = x + modp_ref[0, :, 5 * d:6 * d] * ybuf[i % 2].reshape(tm, d)
        xo_ref[...] = x

    ms = jnp.mean(x * x, axis=-1, keepdims=True)
    y = x * lax.rsqrt(ms + NORM_EPS) * g_ref[...]
    sh = mod_ref[0, :, 0:d]
    sc = mod_ref[0, :, d:2 * d]
    h = (y * (1.0 + sc) + sh).astype(BF16)

    cos = cos_ref[...]
    sin = sin_ref[...]
    hm = hm_ref[...]
    lane = lax.broadcasted_iota(jnp.int32, (1, LANES), 1)
    first_quarter = (lane % 32) < 16

    def norm_rope(z, gain, scale):
        hi, lo = _split(z * z)
        msq = _dot(hi, hm) + _dot(lo, hm)
        zn = z * lax.rsqrt(msq + NORM_EPS) * gain
        rot = jnp.where(first_quarter, -pltpu.roll(zn, LANES - 16, 1), pltpu.roll(zn, 16, 1))
        return (zn * cos + rot * sin) * scale

    chunk = 1024
    n_gate_chunks = zg_ref.shape[1] // chunk
    parts = 4 + n_gate_chunks
    zq = _dot(h, w_ref[:, 0:ATTN_W])
    for cb in range(ATTN_W // LANES):
        sl = slice(cb * LANES, (cb + 1) * LANES)
        q_ref[:, sl] = norm_rope(zq[:, sl], qg_ref[...], Q_SCALE).astype(BF16)
    issue_next(0, parts)
    zkv = _dot(h, w_ref[:, ATTN_W:ATTN_W + 2 * KV_W])
    k_ref[...] = norm_rope(zkv[:, 0:KV_W], kg_ref[...], 1.0).astype(BF16)
    v_ref[...] = zkv[:, KV_W:2 * KV_W].astype(BF16)
    issue_next(1, parts)
    off = ATTN_W + 2 * KV_W
    for part, ref in ((2, zhy_ref), (3, zcf_ref)):
        w = ref.shape[1]
        ref[...] = _dot(h, w_ref[:, off:off + w])
        off += w
        issue_next(part, parts)
    for cb in range(n_gate_chunks):
        zg_ref[:, cb * chunk:(cb + 1) * chunk] = _dot(
            h, w_ref[:, off + cb * chunk:off + (cb + 1) * chunk])
        issue_next(4 + cb, parts)

    if pending:
        @pl.when(i == last)
        def _():
            _row_copies(groups, nxt, "wait")


def _inproj_call(x, mod, g1, w_in, cos, sin, qg, kg, hm, pending=None, *, tm, hy_w, cf_w):
    t, d = x.shape
    tiles_per_mod = t // mod.shape[0] // tm
    n = w_in.shape[1]
    g_w = n - ATTN_W - 2 * KV_W - hy_w - cf_w
    n_pos = cos.shape[0] // tm
    row = lambda i, *_: (i, 0)
    const = lambda i, *_: (0, 0)
    mod_spec = pl.BlockSpec((1, 1, mod.shape[2]), lambda i, *_: (i // tiles_per_mod, 0, 0))
    in_specs = [pl.BlockSpec((tm, d), row), mod_spec,
                pl.BlockSpec((1, d), const),
                pl.BlockSpec((d, n), const),
                pl.BlockSpec((tm, LANES), lambda i, *_: (i % n_pos, 0)),
                pl.BlockSpec((tm, LANES), lambda i, *_: (i % n_pos, 0)),
                pl.BlockSpec((1, LANES), const),
                pl.BlockSpec((1, LANES), const),
                pl.BlockSpec((LANES, LANES), const)]
    out_specs = [pl.BlockSpec((tm, ATTN_W), row),
                 pl.BlockSpec((tm, KV_W), row),
                 pl.BlockSpec((tm, KV_W), row),
                 pl.BlockSpec((tm, hy_w), row),
                 pl.BlockSpec((tm, cf_w), row),
                 pl.BlockSpec((tm, g_w), row)]
    out_shape = [jax.ShapeDtypeStruct((t, ATTN_W), BF16),
                 jax.ShapeDtypeStruct((t, KV_W), BF16),
                 jax.ShapeDtypeStruct((t, KV_W), BF16),
                 jax.ShapeDtypeStruct((t, hy_w), F32),
                 jax.ShapeDtypeStruct((t, cf_w), F32),
                 jax.ShapeDtypeStruct((t, g_w), F32)]
    args = [x, mod, g1, w_in, cos, sin, qg, kg, hm]
    scratch = []
    if pending is not None:
        pos, mod_prev, ys = pending
        in_specs += [mod_spec, pl.BlockSpec(memory_space=pl.ANY)]
        out_specs.append(pl.BlockSpec((tm, d), row))
        out_shape.append(jax.ShapeDtypeStruct((t, d), F32))
        args = [pos] + args + [mod_prev, _grouped(ys)]
        scratch = [pltpu.VMEM((2, tm // SUBLANES, SUBLANES, d), F32), pltpu.SemaphoreType.DMA((2,))]
    return pl.pallas_call(
        functools.partial(_inproj_kernel, pending=pending is not None, n_steps=t // tm),
        grid_spec=pltpu.PrefetchScalarGridSpec(
            num_scalar_prefetch=0 if pending is None else 1,
            grid=(t // tm,),
            in_specs=in_specs, out_specs=out_specs, scratch_shapes=scratch),
        out_shape=out_shape,
        compiler_params=_cparams("arbitrary"),
        name="in_proj",
    )(*args)


def _attn_kernel(*refs, n_src):
    q_ref = refs[0]
    kv_refs = refs[1:1 + 2 * n_src]
    o_ref = refs[1 + 2 * n_src]
    kk_ref, vv_ref = refs[2 + 2 * n_src:]
    h = pl.program_id(1)
    lane = lax.broadcasted_iota(jnp.int32, (1, LANES), 1)
    lo_half = lane < HEAD_DIM

    @pl.when(pl.program_id(2) == 0)
    def _():
        keep = (lane // HEAD_DIM) == h
        off = 0
        for s in range(n_src):
            n = kv_refs[2 * s].shape[0]
            k = kv_refs[2 * s][...].astype(F32)
            kk_ref[off:off + n, :] = jnp.where(keep, k, pltpu.roll(k, HEAD_DIM, 1)).astype(BF16)
            v = kv_refs[2 * s + 1][...].astype(F32)
            v = jnp.where(keep, v, pltpu.roll(v, HEAD_DIM, 1))
            vv_ref[off:off + n, :] = jnp.where(lo_half, v, 1.0).astype(BF16)
            off += n

    kk = kk_ref[...]
    vv = vv_ref[...]
    n_heads = 2 * (q_ref.shape[1] // LANES)

    def scores(hd):
        q2 = q_ref[:, (hd // 2) * LANES:(hd // 2 + 1) * LANES]
        msk = lo_half if hd % 2 == 0 else jnp.logical_not(lo_half)
        qm = jnp.where(msk, q2, jnp.zeros_like(q2))
        return lax.dot_general(qm, kk, (((1,), (1,)), ((), ())), preferred_element_type=F32)

    def probs(s):
        return jnp.exp2(s - jnp.max(s, axis=-1, keepdims=True)).astype(BF16)

    def values(p):
        o = _dot(p, vv)
        return o / o[:, HEAD_DIM:HEAD_DIM + 1]

    s_all = {0: scores(0)}
    outs = {}
    p_prev = None
    for hd in range(n_heads):
        if hd + 1 < n_heads:
            s_all[hd + 1] = scores(hd + 1)
        p_cur = probs(s_all.pop(hd))
        if p_prev is not None:
            outs[hd - 1] = values(p_prev)
        p_prev = p_cur
    outs[n_heads - 1] = values(p_prev)
    for j in range(n_heads // 2):
        pair = jnp.where(lo_half, outs[2 * j], pltpu.roll(outs[2 * j + 1], HEAD_DIM, 1))
        o_ref[:, j * LANES:(j + 1) * LANES] = pair.astype(BF16)


def _attn_call(q, kvs, *, batch, tq):
    t = q.shape[0]
    nq = t // batch // tq
    gw = ATTN_W // N_KV_HEADS
    in_specs = [pl.BlockSpec((tq, gw), lambda b, h, i: (b * nq + i, h))]
    args = [q]
    nk = 0
    for k, v in kvs:
        n = k.shape[0] // batch
        nk += n
        for a in (k, v):
            in_specs.append(pl.BlockSpec((n, KV_W), lambda b, h, i: (b, 0)))
            args.append(a)
    return pl.pallas_call(
        functools.partial(_attn_kernel, n_src=len(kvs)),
        grid=(batch, N_KV_HEADS, nq),
        in_specs=in_specs,
        out_specs=pl.BlockSpec((tq, gw), lambda b, h, i: (b * nq + i, h)),
        out_shape=jax.ShapeDtypeStruct((t, ATTN_W), BF16),
        scratch_shapes=[pltpu.VMEM((nk, LANES), BF16), pltpu.VMEM((nk, LANES), BF16)],
        compiler_params=_cparams("arbitrary", "arbitrary", "arbitrary"),
        name="attention",
    )(*args)


def _hy_time_kernel(feat_ref, w1_ref, b1_ref, w2_ref, b2_ref, w3_ref, fq_ref, dec_ref,
                    sp_ref, sm_ref, kn_ref):
    n = feat_ref.shape[0]
    fq = fq_ref[0]
    h = jnp.sin(fq * (_dot3(feat_ref[...], w1_ref[0]) + b1_ref[0]))
    h = jnp.sin(fq * (_dot3(h, w2_ref[0]) + b2_ref[0]))
    h = _dot3(h, w3_ref[0]) * dec_ref[...]
    h = h / jnp.sum(jnp.abs(h), axis=0, keepdims=True)
    c = h.shape[1] // 2
    row = lax.broadcasted_iota(jnp.int32, (n, 1), 0)
    hf = h[:, 0:c]
    hb = jnp.where(row == 0, 0.0, h[:, c:2 * c])
    sp = hf + hb
    sp_ref[0] = sp
    sm_ref[0] = hf - hb
    sgn = (1 - 2 * (row & 1)).astype(F32)
    kn_ref[0] = jnp.sum(sp * sgn, axis=0, keepdims=True) * (0.5 / n)


def _hy_spec_kernel(fch_ref, fcl_ref, fsh_ref, fsl_ref, sp_ref, sm_ref, kr_ref, ki_ref, *, n):
    tk = fch_ref.shape[0]
    k = pl.program_id(1) * tk + lax.broadcasted_iota(jnp.int32, (tk, 1), 0)
    scl = jnp.where(k == 0, 0.5 / n, 1.0 / n)
    sph, spl = _split(sp_ref[0])
    smh, sml = _split(sm_ref[0])
    kr = _dot(fch_ref[...], sph) + (_dot(fcl_ref[...], sph) + _dot(fch_ref[...], spl))
    ki = _dot(fsh_ref[...], smh) + (_dot(fsl_ref[...], smh) + _dot(fsh_ref[...], sml))
    kr_ref[0] = kr * scl
    ki_ref[0] = ki * scl


def _dft_tables(n):
    k = lax.broadcasted_iota(jnp.int32, (n, n), 0)
    t = lax.broadcasted_iota(jnp.int32, (n, n), 1)
    ang = ((k * t) % (2 * n)).astype(F32) * (math.pi / n)
    return _split(jnp.cos(ang)) + _split(-jnp.sin(ang))


def _hyena_features(n, c):
    t = jnp.linspace(0.0, 1.0, n, dtype=F32)[:, None]
    w = 2.0 * math.pi * jnp.arange(n, dtype=F32)[:, None] / n
    f = jnp.linspace(1e-4, HYENA_BANDS - 1, HYENA_BANDS, dtype=F32)[None, :]
    feat = jnp.concatenate([t, jnp.cos(f * w), -jnp.sin(f * w)], axis=-1)
    feat = jnp.pad(feat, ((0, 0), (0, LANES - feat.shape[1])))
    max_decay = math.log(HYENA_TARGET) / HYENA_FAST_DECAY
    min_decay = math.log(HYENA_TARGET) / HYENA_SLOW_DECAY
    deltas = jnp.tile(jnp.linspace(min_decay, max_decay, c, dtype=F32), 2)
    return feat, jnp.exp(-t * jnp.abs(deltas))


def _hyena_filter_call(n, tabs, w1p, b1, w2, b2, w3, fq):
    depth = w1p.shape[0]
    ffn = w2.shape[1]
    c = w3.shape[2] // 2
    feat, dec = _hyena_features(n, c)
    lay = lambda l: (l, 0, 0)
    const = lambda l: (0, 0)
    sp, sm, kn = pl.pallas_call(
        _hy_time_kernel,
        grid=(depth,),
        in_specs=[pl.BlockSpec((n, LANES), const),
                  pl.BlockSpec((1, LANES, ffn), lay), pl.BlockSpec((1, 1, ffn), lay),
                  pl.BlockSpec((1, ffn, ffn), lay), pl.BlockSpec((1, 1, ffn), lay),
                  pl.BlockSpec((1, ffn, 2 * c), lay), pl.BlockSpec((1, 1, ffn), lay),
                  pl.BlockSpec((n, 2 * c), const)],
        out_specs=[pl.BlockSpec((1, n, c), lay), pl.BlockSpec((1, n, c), lay),
                   pl.BlockSpec((1, 1, c), lay)],
        out_shape=[jax.ShapeDtypeStruct((depth, n, c), F32), jax.ShapeDtypeStruct((depth, n, c), F32),
                   jax.ShapeDtypeStruct((depth, 1, c), F32)],
        compiler_params=_cparams("arbitrary"),
        name="hyena_filter_time",
    )(feat, w1p, b1, w2, b2, w3, fq, dec)
    tk = min(n, 512)
    mat = pl.BlockSpec((tk, n), lambda l, j: (j, 0))
    vec = pl.BlockSpec((1, n, c), lambda l, j: (l, 0, 0))
    out = pl.BlockSpec((1, tk, c), lambda l, j: (l, j, 0))
    kr, ki = pl.pallas_call(
        functools.partial(_hy_spec_kernel, n=n),
        grid=(depth, n // tk),
        in_specs=[mat, mat, mat, mat, vec, vec],
        out_specs=[out, out],
        out_shape=[jax.ShapeDtypeStruct((depth, n, c), F32)] * 2,
        compiler_params=_cparams("arbitrary", "arbitrary"),
        name="hyena_filter_spectrum",
    )(*tabs, sp, sm)
    return kr, ki, kn


HYENA_ROW_CHUNK = 512


def _hyena_kernel(z_ref, sw_ref, sb_ref, fc_ref, fs_ref, kr_ref, ki_ref, kn_ref, db_ref, o_ref):
    n = z_ref.shape[0]
    c = o_ref.shape[1]
    z = z_ref[...]
    row = lax.broadcasted_iota(jnp.int32, (n, 1), 0)
    z_prev = jnp.where(row == 0, 0.0, pltpu.roll(z, 1, 0))
    z_next = jnp.where(row == n - 1, 0.0, pltpu.roll(z, n - 1, 0))
    zc = z_prev * sw_ref[0:1, :] + z * sw_ref[1:2, :] + z_next * sw_ref[2:3, :] + sb_ref[...]
    x0 = zc[:, 0:c]
    u = zc[:, 2 * c:3 * c] * zc[:, c:2 * c]
    ub = u.astype(BF16)
    sgn = (1 - 2 * (row & 1)).astype(F32)
    nyq = jnp.sum(u * sgn, axis=0, keepdims=True) * kn_ref[...]
    chunk = min(n, HYENA_ROW_CHUNK)
    chunks = [slice(r, r + chunk) for r in range(0, n, chunk)]
    pr, pi = [], []
    for rs in chunks:
        ur = _dot(fc_ref[rs, :], ub)
        ui = _dot(fs_ref[rs, :], ub)
        kr = kr_ref[rs, :]
        ki = ki_ref[rs, :]
        pr.append((ur * kr - ui * ki).astype(BF16))
        pi.append((ur * ki + ui * kr).astype(BF16))
    pr = jnp.concatenate(pr, axis=0)
    pi = jnp.concatenate(pi, axis=0)
    for rs in chunks:
        y = _dot(fc_ref[rs, :], pr) + _dot(fs_ref[rs, :], pi) + sgn[rs, :] * nyq
        o_ref[rs, :] = ((y + u[rs, :] * db_ref[...]) * x0[rs, :]).astype(BF16)


def _hyena_call(z, sw, sb, fc, fs, kr, ki, kn, db, *, n):
    t, w = z.shape
    c = w // 3
    const = lambda b: (0, 0)
    return pl.pallas_call(
        _hyena_kernel,
        grid=(t // n,),
        in_specs=[pl.BlockSpec((n, w), lambda b: (b, 0)),
                  pl.BlockSpec((3, w), const), pl.BlockSpec((1, w), const),
                  pl.BlockSpec((n, n), const), pl.BlockSpec((n, n), const),
                  pl.BlockSpec((n, c), const), pl.BlockSpec((n, c), const),
                  pl.BlockSpec((1, c), const), pl.BlockSpec((1, c), const)],
        out_specs=pl.BlockSpec((n, c), lambda b: (b, 0)),
        out_shape=jax.ShapeDtypeStruct((t, c), BF16),
        compiler_params=_cparams("arbitrary"),
        name="hyena_conv",
    )(z, sw, sb, fc, fs, kr, ki, kn, db)


CONF_PAD = 16


def _conf_kernel(z_ref, w_ref, b_ref, g_ref, beta_ref, o_ref, pad_ref):
    n = z_ref.shape[0]
    c = o_ref.shape[1]
    u = z_ref[:, 0:c] * jax.nn.sigmoid(z_ref[:, c:2 * c])
    zeros = jnp.zeros((CONF_PAD, c), F32)
    pad_ref[0:CONF_PAD, :] = zeros
    pad_ref[CONF_PAD + n:2 * CONF_PAD + n, :] = zeros
    pad_ref[CONF_PAD:CONF_PAD + n, :] = u
    first = CONF_PAD - (CONF_K - 1) // 2
    acc = b_ref[...]
    for r in range(SUBLANES):
        part = None
        for off in range(r, first + CONF_K, SUBLANES):
            j = off - first
            if j < 0:
                continue
            term = w_ref[j:j + 1, :] * pad_ref[off - r:off - r + n + SUBLANES, :]
            part = term if part is None else part + term
        acc = acc + part[r:r + n, :]
    mu = jnp.mean(acc, axis=-1, keepdims=True)
    dlt = acc - mu
    var = jnp.mean(dlt * dlt, axis=-1, keepdims=True)
    y = dlt * lax.rsqrt(var + NORM_EPS) * g_ref[...] + beta_ref[...]
    o_ref[...] = _silu(y).astype(BF16)


def _conf_call(z, w, b, g, beta, *, n):
    t, w2 = z.shape
    c = w2 // 2
    const = lambda i: (0, 0)
    return pl.pallas_call(
        _conf_kernel,
        grid=(t // n,),
        in_specs=[pl.BlockSpec((n, w2), lambda i: (i, 0)),
                  pl.BlockSpec((CONF_K, c), const), pl.BlockSpec((1, c), const),
                  pl.BlockSpec((1, c), const), pl.BlockSpec((1, c), const)],
        out_specs=pl.BlockSpec((n, c), lambda i: (i, 0)),
        out_shape=jax.ShapeDtypeStruct((t, c), BF16),
        scratch_shapes=[pltpu.VMEM((n + 2 * CONF_PAD, c), F32)],
        compiler_params=_cparams("arbitrary"),
        name="conformer_conv",
    )(z, w, b, g, beta)


def _route(sel, s):
    grp_score = []
    for g in range(N_GROUPS):
        r = sel[g * GROUP_SIZE:(g + 1) * GROUP_SIZE]
        best = None
        for i in range(GROUP_SIZE):
            for j in range(i + 1, GROUP_SIZE):
                pair = r[i] + r[j]
                best = pair if best is None else jnp.maximum(best, pair)
        grp_score.append(best)
    gbest = grp_score[0]
    gidx = jnp.zeros_like(gbest, dtype=jnp.int32)
    for g in range(1, N_GROUPS):
        better = grp_score[g] > gbest
        gidx = jnp.where(better, g, gidx)
        gbest = jnp.where(better, grp_score[g], gbest)
    neg = jnp.full_like(gbest, -jnp.inf)
    masked = [jnp.where(gidx == e // GROUP_SIZE, sel[e], neg) for e in range(N_EXPERTS)]

    def argmax_first(vals):
        best = vals[0]
        idx = jnp.zeros_like(gidx)
        for e in range(1, N_EXPERTS):
            better = vals[e] > best
            idx = jnp.where(better, e, idx)
            best = jnp.where(better, vals[e], best)
        return idx

    i1 = argmax_first(masked)
    i2 = argmax_first([jnp.where(i1 == e, neg, masked[e]) for e in range(N_EXPERTS)])
    zero = jnp.zeros_like(gbest)
    w1 = sum(jnp.where(i1 == e, s[e], zero) for e in range(N_EXPERTS))
    w2 = sum(jnp.where(i2 == e, s[e], zero) for e in range(N_EXPERTS))
    tot = w1 + w2
    w1, w2 = w1 / tot, w2 / tot
    swap = i2 < i1
    a = jnp.where(swap, i2, i1) - GROUP_SIZE * gidx
    b = jnp.where(swap, i1, i2) - GROUP_SIZE * gidx
    cls = PAIRS_PER_GROUP * gidx + ((a * (2 * GROUP_SIZE - 1 - a)) >> 1) + (b - a - 1)
    return cls, jnp.where(swap, w2, w1), jnp.where(swap, w1, w2)


PAIRS_PER_GROUP = GROUP_SIZE * (GROUP_SIZE - 1) // 2
N_CLASSES = N_GROUPS * PAIRS_PER_GROUP
SORT_TILE = 512


def _class_experts():
    lo, hi = [], []
    for g in range(N_GROUPS):
        for a in range(GROUP_SIZE):
            for b in range(a + 1, GROUP_SIZE):
                lo.append(g * GROUP_SIZE + a)
                hi.append(g * GROUP_SIZE + b)
    return jnp.asarray(lo + hi, dtype=jnp.int32)


def _col(row):
    return jnp.broadcast_to(row, (LANES, row.shape[1])).T[:, 0:1]


def _row(col):
    return jnp.broadcast_to(col, (col.shape[0], LANES)).T[0:1, :]


STATE_ROWS = 8
STATE_LANES = 256


def _initial_state():
    lane = lax.broadcasted_iota(jnp.int32, (1, STATE_LANES), 1)
    zero = jnp.zeros((1, STATE_LANES), jnp.int32)
    rows = [zero, jnp.where(lane < N_CLASSES, lane, 0), zero + N_CLASSES,
            jnp.where(lane < N_CLASSES, lane, LANES)] + [zero] * (STATE_ROWS - 4)
    return jnp.concatenate(rows, axis=0).astype(F32)


def _allocate_rows(cls, st_ref):
    tm = cls.shape[1]
    cls_col = _col(cls.astype(F32)).astype(jnp.int32)
    lane = lax.broadcasted_iota(jnp.int32, (1, LANES), 1)
    onehot = (cls_col == lane).astype(F32)
    counts = jnp.sum(onehot, axis=0, keepdims=True)
    t = lax.broadcasted_iota(jnp.int32, (tm, tm), 0)
    s = lax.broadcasted_iota(jnp.int32, (tm, tm), 1)
    earlier = _dot((s < t).astype(BF16), onehot.astype(BF16))
    fill = st_ref[0:1, 0:LANES]
    cur = st_ref[1:2, 0:LANES]
    nfree = st_ref[2:3, 0:LANES]
    tot = fill + counts
    full = tot >= SORT_TILE
    full_f = full.astype(F32)
    r = lax.broadcasted_iota(jnp.int32, (LANES, LANES), 0)
    c = lax.broadcasted_iota(jnp.int32, (LANES, LANES), 1)
    before = _dot(jnp.broadcast_to(full_f, (SUBLANES, LANES)).astype(BF16), (r < c).astype(BF16))[0:1, :]
    newid = nfree + before
    rank = jnp.sum(onehot * (fill + earlier), axis=1, keepdims=True)
    cur_base = jnp.sum(onehot * cur, axis=1, keepdims=True) * SORT_TILE
    new_base = jnp.sum(onehot * newid, axis=1, keepdims=True) * SORT_TILE
    pos_col = jnp.where(rank < SORT_TILE, cur_base + rank, new_base + (rank - SORT_TILE))
    st_ref[0:1, 0:LANES] = jnp.where(full, tot - SORT_TILE, tot)
    st_ref[1:2, 0:LANES] = jnp.where(full, newid, cur)
    st_ref[2:3, :] = st_ref[2:3, :] + jnp.sum(full_f, axis=1, keepdims=True)
    chunk = lax.broadcasted_iota(jnp.int32, (1, STATE_LANES), 1).astype(F32)
    opened = jnp.logical_and(_col(newid) == chunk, _col(full_f) > 0.0)
    cls_sub = lax.broadcasted_iota(jnp.int32, (LANES, 1), 0).astype(F32)
    chunk_cls = jnp.sum(jnp.where(opened, cls_sub, 0.0), axis=0, keepdims=True)
    hit = jnp.sum(opened.astype(F32), axis=0, keepdims=True)
    st_ref[3:4, :] = jnp.where(hit > 0.0, chunk_cls, st_ref[3:4, :])
    return _row(pos_col).astype(jnp.int32)


def _merge_kernel(oa_ref, oh_ref, oc_ref, zg_ref, x_ref, mod_ref, g2_ref,
                  pa_ref, ph_ref, pc_ref, wo_ref, wr_ref, br_ref, state_ref, buf_in,
                  xo_ref, pos_ref, state_out_ref, buf_out,
                  hxs, posv, poss, st, row_sem, pos_sem, *, n_steps):
    del buf_in
    tm, d = x_ref.shape
    groups = tm // SUBLANES
    i = pl.program_id(0)
    last = n_steps - 1
    slot = i % 2
    other = (i + 1) % 2
    dump_row = buf_out.shape[0] * SUBLANES - tm

    def row_copy(sl):
        def build(g, j):
            return pltpu.make_async_copy(hxs.at[sl, g, pl.ds(j, 1)],
                                         _row_at(buf_out, poss[sl, g * SUBLANES + j]), row_sem.at[sl])
        return build

    def pos_copy(sl):
        return pltpu.make_async_copy(posv.at[0], poss.at[sl], pos_sem)

    @pl.when(i == 0)
    def _():
        st[...] = state_ref[...]
        hxs[1] = jnp.zeros(hxs.shape[1:], F32)

        def fill_dump(r, carry):
            poss[1, r] = dump_row + r
            return carry

        lax.fori_loop(0, tm, fill_dump, 0)

    @pl.when(i >= 1)
    def _():
        pos_copy(other).wait()
        _row_copies(groups, row_copy(slot), "wait")

    send_prev = row_copy(other)

    def issue_prev(part, parts):
        for g in range(groups * part // parts, groups * (part + 1) // parts):
            for j in range(SUBLANES):
                send_prev(g, j).start(priority=j % 2)

    merged = jax.nn.sigmoid(zg_ref[:, 0:d]) * _dot(oa_ref[...], pa_ref[...])
    issue_prev(0, 4)
    merged += jax.nn.sigmoid(zg_ref[:, d:2 * d]) * _dot(oh_ref[...], ph_ref[...])
    issue_prev(1, 4)
    merged += jax.nn.sigmoid(zg_ref[:, 2 * d:3 * d]) * _dot(oc_ref[...], pc_ref[...])
    issue_prev(2, 4)
    y = _dot(merged.astype(BF16), wo_ref[...])
    issue_prev(3, 4)
    x = x_ref[...] + mod_ref[0, :, 2 * d:3 * d] * y
    xo_ref[...] = x
    ms = jnp.mean(x * x, axis=-1, keepdims=True)
    h2 = x * lax.rsqrt(ms + NORM_EPS) * g2_ref[...]
    h2 = h2 * (1.0 + mod_ref[0, :, 4 * d:5 * d]) + mod_ref[0, :, 3 * d:4 * d]
    hxs[slot, :, :, 0:d] = h2.reshape(groups, SUBLANES, d)
    hh, hl = _split(h2)
    wh, wl = _split(wr_ref[...])
    nt = (((1,), (1,)), ((), ()))
    logits = (lax.dot_general(wh, hh, nt, preferred_element_type=F32)
              + (lax.dot_general(wh, hl, nt, preferred_element_type=F32)
                 + lax.dot_general(wl, hh, nt, preferred_element_type=F32)))
    s = jax.nn.sigmoid(logits)
    sel = s + br_ref[...]
    cls, w_lo, w_hi = _route([sel[e:e + 1, :] for e in range(N_EXPERTS)],
                             [s[e:e + 1, :] for e in range(N_EXPERTS)])
    sub = lax.broadcasted_iota(jnp.int32, (LANES, tm), 0)
    rows = jnp.where(sub == 0, w_lo, jnp.where(sub == 1, w_hi, 0.0))
    hxs[slot, :, :, d:d + LANES] = rows.T.reshape(groups, SUBLANES, LANES)
    pos = _allocate_rows(cls, st)
    pos_ref[0] = pos
    posv[...] = pos
    pos_copy(slot).start()

    @pl.when(i == last)
    def _():
        pos_copy(slot).wait()
        _row_copies(groups, row_copy(slot), "start")
        _row_copies(groups, send_prev, "wait")
        _row_copies(groups, row_copy(slot), "wait")
        state_out_ref[...] = st[...]


def _merge_call(oa, oh, oc, zg, x, mod, g2, pa, ph, pc, wo, wr_t, br, state, buf, *, tm):
    t, d = x.shape
    w = d + LANES
    tiles_per_mod = t // mod.shape[0] // tm
    groups = tm // SUBLANES
    row = lambda i: (i, 0)
    const = lambda i: (0, 0)
    full = lambda a: pl.BlockSpec(a.shape, const)
    any_spec = pl.BlockSpec(memory_space=pl.ANY)
    x1, pos, state, buf = pl.pallas_call(
        functools.partial(_merge_kernel, n_steps=t // tm),
        grid=(t // tm,),
        in_specs=[pl.BlockSpec((tm, oa.shape[1]), row), pl.BlockSpec((tm, oh.shape[1]), row),
                  pl.BlockSpec((tm, oc.shape[1]), row), pl.BlockSpec((tm, zg.shape[1]), row),
                  pl.BlockSpec((tm, d), row),
                  pl.BlockSpec((1, 1, mod.shape[2]), lambda i: (i // tiles_per_mod, 0, 0)),
                  full(g2), full(pa), full(ph), full(pc), full(wo), full(wr_t), full(br),
                  full(state), any_spec],
        out_specs=[pl.BlockSpec((tm, d), row), pl.BlockSpec((1, 1, tm), lambda i: (i, 0, 0)),
                   full(state), any_spec],
        out_shape=[jax.ShapeDtypeStruct((t, d), F32), jax.ShapeDtypeStruct((t // tm, 1, tm), jnp.int32),
                   jax.ShapeDtypeStruct(state.shape, F32),
                   jax.ShapeDtypeStruct((buf.shape[0] // SUBLANES, SUBLANES, w), F32)],
        input_output_aliases={14: 3},
        scratch_shapes=[pltpu.VMEM((2, groups, SUBLANES, w), F32), pltpu.VMEM((1, tm), jnp.int32),
                        pltpu.SMEM((2, tm), jnp.int32), pltpu.VMEM(state.shape, F32),
                        pltpu.SemaphoreType.DMA((2,)), pltpu.SemaphoreType.DMA],
        compiler_params=_cparams("arbitrary"),
        name="merge_router",
    )(oa, oh, oc, zg, x, mod, g2, pa, ph, pc, wo, wr_t, br, state, _grouped(buf))
    return x1, pos.reshape(t), state, buf.reshape(-1, w)


def _ffn_kernel(tcls_ref, etab_ref, xs_ref, wga_ref, wua_ref, wda_ref, wgb_ref, wub_ref, wdb_ref, o_ref):
    del etab_ref
    d = o_ref.shape[1]
    valid = tcls_ref[pl.program_id(0)] < N_CLASSES

    @pl.when(valid)
    def _():
        x = xs_ref[:, 0:d].astype(BF16)
        ha = _silu(_dot(x, wga_ref[0])) * _dot(x, wua_ref[0]) * xs_ref[:, d:d + 1]
        hb = _silu(_dot(x, wgb_ref[0])) * _dot(x, wub_ref[0]) * xs_ref[:, d + 1:d + 2]
        o_ref[...] = _dot(ha.astype(BF16), wda_ref[0]) + _dot(hb.astype(BF16), wdb_ref[0])

    @pl.when(jnp.logical_not(valid))
    def _():
        o_ref[...] = jnp.zeros_like(o_ref)


def _ffn_call(tcls, etab, buf, wg, wu, wd):
    rows, w = tcls.shape[0] * SORT_TILE, buf.shape[1]
    ne, d, ff = wg.shape

    def expert(which):
        def index(j, tcls_ref, etab_ref):
            c = jnp.minimum(tcls_ref[j], N_CLASSES - 1)
            return (etab_ref[which * N_CLASSES + c], 0, 0)
        return index

    up = lambda which: pl.BlockSpec((1, d, ff), expert(which))
    down = lambda which: pl.BlockSpec((1, ff, d), expert(which))
    return pl.pallas_call(
        _ffn_kernel,
        grid_spec=pltpu.PrefetchScalarGridSpec(
            num_scalar_prefetch=2,
            grid=(rows // SORT_TILE,),
            in_specs=[pl.BlockSpec((SORT_TILE, w), lambda j, *_: (j, 0)),
                      up(0), up(0), down(0), up(1), up(1), down(1)],
            out_specs=pl.BlockSpec((SORT_TILE, d), lambda j, *_: (j, 0))),
        out_shape=jax.ShapeDtypeStruct((rows, d), F32),
        compiler_params=_cparams("arbitrary"),
        name="moe_ffn",
    )(tcls, etab, buf, wg, wu, wd, wg, wu, wd)


def _combine_kernel(pos_ref, x_ref, mod_ref, ys_hbm, o_ref, ybuf, sem, *, n_steps):
    tm, d = x_ref.shape
    groups = tm // SUBLANES
    i = pl.program_id(0)

    def gather(tile, op):
        slot = tile % 2

        def copy_of_row(g, j):
            p = pos_ref[tile * tm + g * SUBLANES + j]
            return pltpu.make_async_copy(_row_at(ys_hbm, p), ybuf.at[slot, g, pl.ds(j, 1)],
                                         sem.at[slot])

        _row_copies(groups, copy_of_row, op)

    @pl.when(i == 0)
    def _():
        gather(i, "start")

    @pl.when(i + 1 < n_steps)
    def _():
        gather(i + 1, "start")

    gather(i, "wait")
    o_ref[...] = x_ref[...] + mod_ref[0, :, 5 * d:6 * d] * ybuf[i % 2].reshape(tm, d)


def _combine_call(pos, x, mod, ys, *, tm):
    t, d = x.shape
    tiles_per_mod = t // mod.shape[0] // tm
    return pl.pallas_call(
        functools.partial(_combine_kernel, n_steps=t // tm),
        grid_spec=pltpu.PrefetchScalarGridSpec(
            num_scalar_prefetch=1,
            grid=(t // tm,),
            in_specs=[pl.BlockSpec((tm, d), lambda i, *_: (i, 0)),
                      pl.BlockSpec((1, 1, mod.shape[2]), lambda i, *_: (i // tiles_per_mod, 0, 0)),
                      pl.BlockSpec(memory_space=pl.ANY)],
            out_specs=pl.BlockSpec((tm, d), lambda i, *_: (i, 0)),
            scratch_shapes=[pltpu.VMEM((2, tm // SUBLANES, SUBLANES, d), F32),
                            pltpu.SemaphoreType.DMA((2,))]),
        out_shape=jax.ShapeDtypeStruct((t, d), F32),
        compiler_params=_cparams("arbitrary"),
        name="moe_combine",
    )(pos, x, mod, _grouped(ys))


def _rope_tables(n_tokens):
    rows = n_tokens // GRID_W
    row_ids = jnp.repeat(jnp.arange(rows), GRID_W).astype(F32)
    col_ids = jnp.tile(jnp.arange(GRID_W), rows).astype(F32)
    half = HEAD_DIM // 2
    inv = ROPE_THETA ** (-jnp.arange(0, half, 2, dtype=F32) / half)
    ang_r = row_ids[:, None] * inv
    ang_c = col_ids[:, None] * inv
    ang = jnp.concatenate([ang_r, ang_r, ang_c, ang_c] * (LANES // HEAD_DIM), axis=-1)
    return jnp.cos(ang), jnp.sin(ang)


def _head_mean_matrix():
    i = lax.broadcasted_iota(jnp.int32, (LANES, LANES), 0) // HEAD_DIM
    j = lax.broadcasted_iota(jnp.int32, (LANES, LANES), 1) // HEAD_DIM
    return jnp.where(i == j, 1.0 / HEAD_DIM, 0.0).astype(BF16)


def kernel(x, c, ctx, c_ctx, w_mod, b_mod, g_norm1, g_norm2, w_in, q_norm, k_norm, hy_short_w, hy_short_b, hy_w1, hy_b1, hy_w2, hy_b2, hy_w3, hy_freq, hy_bias, cf_dw_w, cf_dw_b, cf_ln_g, cf_ln_b, p_attn, p_hyena, p_conv, w_out, w_router, b_router, w_exp_gate, w_exp_up, w_exp_down):
    batch, seq, d = x.shape
    n_ctx = ctx.shape[1]
    depth = w_mod.shape[0]
    hy_w = hy_short_w.shape[2]
    cf_w = 2 * cf_dw_w.shape[2]
    tm = 512

    pad_rows = (-(batch + 1)) % 8
    cc = jnp.concatenate([c, c_ctx[None, :], jnp.zeros((pad_rows, d), F32)], axis=0)
    mods = _mod_call(cc, w_mod, b_mod)

    cos_x, sin_x = _rope_tables(seq)
    cos_c = jnp.ones((tm, LANES), F32)
    sin_c = jnp.zeros((tm, LANES), F32)
    hm = _head_mean_matrix()
    tile2 = lambda g: jnp.tile(g, (1, LANES // HEAD_DIM))
    qg, kg = tile2(q_norm), tile2(k_norm)

    w1p = jnp.pad(hy_w1, ((0, 0), (0, LANES - hy_w1.shape[1]), (0, 0)))
    r3 = lambda a: a[:, None, :]
    tabs_x = _dft_tables(seq)
    tabs_c = _dft_tables(n_ctx)
    filt_args = (w1p, r3(hy_b1), hy_w2, r3(hy_b2), hy_w3, r3(hy_freq))
    kr_x, ki_x, kn_x = _hyena_filter_call(seq, tabs_x, *filt_args)
    kr_c, ki_c, kn_c = _hyena_filter_call(n_ctx, tabs_c, *filt_args)

    bf = lambda a: a.astype(BF16)
    w_in_b, pa_b, ph_b, pc_b, wo_b = bf(w_in), bf(p_attn), bf(p_hyena), bf(p_conv), bf(w_out)
    wg_b, wu_b, wd_b = bf(w_exp_gate), bf(w_exp_up), bf(w_exp_down)
    wr_t = w_router.T
    br = b_router[:, None]

    xs = [x.reshape(batch * seq, d), ctx.reshape(batch * n_ctx, d)]
    pending = [None, None]
    for l in range(depth):
        last = l == depth - 1
        mods_l = [mods[l, :batch][:, None, :], mods[l, batch:batch + 1][:, None, :]]
        row = lambda a: a[l][None, :]
        consts = [(seq, cos_x, sin_x, tabs_x, kr_x, ki_x, kn_x),
                  (n_ctx, cos_c, sin_c, tabs_c, kr_c, ki_c, kn_c)]
        proj = []
        for si in range(2):
            out = _inproj_call(xs[si], mods_l[si], row(g_norm1), w_in_b[l], consts[si][1], consts[si][2],
                               qg[l:l + 1], kg[l:l + 1], hm, pending[si], tm=tm, hy_w=hy_w, cf_w=cf_w)
            if pending[si] is not None:
                xs[si] = out[-1]
            proj.append(out[:6])
        (q_x, k_x, v_x, zhy_x, zcf_x, zg_x), (q_c, k_c, v_c, zhy_c, zcf_c, zg_c) = proj
        n_streams = 1 if last else 2
        n_sorted_tiles = sum(a.shape[0] for a in xs[:n_streams]) // SORT_TILE + N_CLASSES
        assert n_sorted_tiles <= STATE_LANES
        state = _initial_state()
        buf = jnp.zeros(((n_sorted_tiles + 1) * SORT_TILE, d + LANES), F32)
        x1s, poss = [], []
        for si in range(n_streams):
            n, _, _, tabs, kr, ki, kn = consts[si]
            if si == 0:
                oa = _attn_call(q_x, [(k_x, v_x), (k_c, v_c)], batch=batch, tq=512)
                zhy, zcf, zg = zhy_x, zcf_x, zg_x
            else:
                oa = _attn_call(q_c, [(k_c, v_c)], batch=batch, tq=n_ctx)
                zhy, zcf, zg = zhy_c, zcf_c, zg_c
            oh = _hyena_call(zhy, hy_short_w[l], row(hy_short_b), tabs[0], tabs[2], kr[l], ki[l], kn[l],
                             row(hy_bias), n=n)
            oc = _conf_call(zcf, cf_dw_w[l], row(cf_dw_b), row(cf_ln_g), row(cf_ln_b), n=n)
            x1, pos, state, buf = _merge_call(oa, oh, oc, zg, xs[si], mods_l[si], row(g_norm2), pa_b[l],
                                              ph_b[l], pc_b[l], wo_b[l], wr_t, br, state, buf, tm=tm)
            x1s.append(x1)
            poss.append(pos)
        tcls = state[3, :n_sorted_tiles].astype(jnp.int32)
        ys = _ffn_call(tcls, _class_experts(), buf, wg_b[l], wu_b[l], wd_b[l])
        if last:
            return _combine_call(poss[0], x1s[0], mods_l[0], ys, tm=tm).reshape(batch, seq, d)
        for si in range(2):
            xs[si] = x1s[si]
            pending[si] = (poss[si], mods_l[si], ys)
```

```python
import functools
import math

import jax
import jax.numpy as jnp
from jax import lax
from jax.experimental import pallas as pl
from jax.experimental.pallas import tpu as pltpu

F32 = jnp.float32
BF16 = jnp.bfloat16

NORM_EPS = 1e-6
N_MOD = 6
HEAD_DIM = 64
N_Q_HEADS = 8
N_KV_HEADS = 2
ATTN_W = N_Q_HEADS * HEAD_DIM
KV_W = N_KV_HEADS * HEAD_DIM
GRID_W = 64
ROPE_THETA = 10000.0
HYENA_BANDS = 16
HYENA_FAST_DECAY = 0.3
HYENA_SLOW_DECAY = 1.5
HYENA_TARGET = 1e-2
CONF_K = 31
N_EXPERTS = 16
N_GROUPS = 4
GROUP_SIZE = N_EXPERTS // N_GROUPS

Q_SCALE = math.log2(math.e) / math.sqrt(HEAD_DIM)

LANES = 128
SUBLANES = 8
VMEM_LIMIT = 56 * 1024 * 1024


def _cparams(*sem):
    return pltpu.CompilerParams(dimension_semantics=sem, vmem_limit_bytes=VMEM_LIMIT)


def _dot(a, b):
    return jnp.dot(a, b, preferred_element_type=F32)


def _split(a):
    hi = a.astype(BF16)
    lo = (a - hi.astype(F32)).astype(BF16)
    return hi, lo


def _dot3(a, b):
    ah, al = _split(a)
    bh, bl = _split(b)
    return _dot(ah, bh) + (_dot(al, bh) + _dot(ah, bl))


def _silu(x):
    return x * jax.nn.sigmoid(x)


def _mod_kernel(c_ref, w_ref, b_ref, o_ref):
    o_ref[0] = _dot3(_silu(c_ref[...]), w_ref[0]) + b_ref[0]


def _mod_call(cc, w_mod, b_mod):
    depth, d, n = w_mod.shape
    r = cc.shape[0]
    tn = 1536
    return pl.pallas_call(
        _mod_kernel,
        grid=(depth, n // tn),
        in_specs=[pl.BlockSpec((r, d), lambda l, j: (0, 0)),
                  pl.BlockSpec((1, d, tn), lambda l, j: (l, 0, j)),
                  pl.BlockSpec((1, 1, tn), lambda l, j: (l, 0, j))],
        out_specs=pl.BlockSpec((1, r, tn), lambda l, j: (l, 0, j)),
        out_shape=jax.ShapeDtypeStruct((depth, r, n), F32),
        compiler_params=_cparams("arbitrary", "arbitrary"),
        name="adaln_mod",
    )(cc, w_mod, b_mod.reshape(depth, 1, n))


def _row_copies(n_groups, copy_of_row, op):
    def group(g, carry):
        for j in range(SUBLANES):
            cp = copy_of_row(g, j)
            if op == "start":
                cp.start(priority=j % 2)
            else:
                cp.wait()
        return carry

    lax.fori_loop(0, n_groups, group, 0)


def _grouped(a):
    return a.reshape(a.shape[0] // SUBLANES, SUBLANES, a.shape[1])


def _row_at(ref, p):
    shift = SUBLANES.bit_length() - 1
    return ref.at[lax.shift_right_logical(p, shift), pl.ds(p & (SUBLANES - 1), 1)]


def _inproj_kernel(*refs, pending, n_steps):
    if pending:
        (pos_ref, x_ref, mod_ref, g_ref, w_ref, cos_ref, sin_ref, qg_ref, kg_ref, hm_ref,
         modp_ref, ys_hbm, q_ref, k_ref, v_ref, zhy_ref, zcf_ref, zg_ref, xo_ref, ybuf, sem) = refs
    else:
        (x_ref, mod_ref, g_ref, w_ref, cos_ref, sin_ref, qg_ref, kg_ref, hm_ref,
         q_ref, k_ref, v_ref, zhy_ref, zcf_ref, zg_ref) = refs
    tm, d = x_ref.shape
    x = x_ref[...]
    issue_next = lambda part, parts: None
    if pending:
        groups = tm // SUBLANES
        i = pl.program_id(0)
        last = n_steps - 1

        def copy_of_row(tile, slot):
            def build(g, j):
                p = pos_ref[tile * tm + g * SUBLANES + j]
                return pltpu.make_async_copy(_row_at(ys_hbm, p), ybuf.at[slot, g, pl.ds(j, 1)],
                                             sem.at[slot])
            return build

        @pl.when(i == 0)
        def _():
            _row_copies(groups, copy_of_row(i, 0), "start")

        nxt = copy_of_row(jnp.minimum(i + 1, last), (i + 1) % 2)

        def issue_next(part, parts):
            for g in range(groups * part // parts, groups * (part + 1) // parts):
                for j in range(SUBLANES):
                    nxt(g, j).start(priority=j % 2)

        _row_copies(groups, copy_of_row(i, i % 2), "wait")
        x = x + modp_ref[0, :, 5 * d:6 * d] * ybuf[i % 2].reshape(tm, d)
        xo_ref[...] = x

    ms = jnp.mean(x * x, axis=-1, keepdims=True)
    y = x * lax.rsqrt(ms + NORM_EPS) * g_ref[...]
    sh = mod_ref[0, :, 0:d]
    sc = mod_ref[0, :, d:2 * d]
    h = (y * (1.0 + sc) + sh).astype(BF16)

    cos = cos_ref[...]
    sin = sin_ref[...]
    hm = hm_ref[...]
    lane = lax.broadcasted_iota(jnp.int32, (1, LANES), 1)
    first_quarter = (lane % 32) < 16

    def norm_rope(z, gain, scale):
        hi, lo = _split(z * z)
        msq = _dot(hi, hm) + _dot(lo, hm)
        zn = z * lax.rsqrt(msq + NORM_EPS) * gain
        rot = jnp.where(first_quarter, -pltpu.roll(zn, LANES - 16, 1), pltpu.roll(zn, 16, 1))
        return (zn * cos + rot * sin) * scale

    chunk = 1024
    n_gate_chunks = zg_ref.shape[1] // chunk
    parts = 4 + n_gate_chunks
    zq = _dot(h, w_ref[:, 0:ATTN_W])
    for cb in range(ATTN_W // LANES):
        sl = slice(cb * LANES, (cb + 1) * LANES)
        q_ref[:, sl] = norm_rope(zq[:, sl], qg_ref[...], Q_SCALE).astype(BF16)
    issue_next(0, parts)
    zkv = _dot(h, w_ref[:, ATTN_W:ATTN_W + 2 * KV_W])
    k_ref[...] = norm_rope(zkv[:, 0:KV_W], kg_ref[...], 1.0).astype(BF16)
    v_ref[...] = zkv[:, KV_W:2 * KV_W].astype(BF16)
    issue_next(1, parts)
    off = ATTN_W + 2 * KV_W
    for part, ref in ((2, zhy_ref), (3, zcf_ref)):
        w = ref.shape[1]
        ref[...] = _dot(h, w_ref[:, off:off + w])
        off += w
        issue_next(part, parts)
    for cb in range(n_gate_chunks):
        zg_ref[:, cb * chunk:(cb + 1) * chunk] = _dot(
            h, w_ref[:, off + cb * chunk:off + (cb + 1) * chunk])
        issue_next(4 + cb, parts)

    if pending:
        @pl.when(i == last)
        def _():
            _row_copies(groups, nxt, "wait")


def _inproj_call(x, mod, g1, w_in, cos, sin, qg, kg, hm, pending=None, *, tm, hy_w, cf_w):
    t, d = x.shape
    tiles_per_mod = t // mod.shape[0] // tm
    n = w_in.shape[1]
    g_w = n - ATTN_W - 2 * KV_W - hy_w - cf_w
    n_pos = cos.shape[0] // tm
    row = lambda i, *_: (i, 0)
    const = lambda i, *_: (0, 0)
    mod_spec = pl.BlockSpec((1, 1, mod.shape[2]), lambda i, *_: (i // tiles_per_mod, 0, 0))
    in_specs = [pl.BlockSpec((tm, d), row), mod_spec,
                pl.BlockSpec((1, d), const),
                pl.BlockSpec((d, n), const),
                pl.BlockSpec((tm, LANES), lambda i, *_: (i % n_pos, 0)),
                pl.BlockSpec((tm, LANES), lambda i, *_: (i % n_pos, 0)),
                pl.BlockSpec((1, LANES), const),
                pl.BlockSpec((1, LANES), const),
                pl.BlockSpec((LANES, LANES), const)]
    out_specs = [pl.BlockSpec((tm, ATTN_W), row),
                 pl.BlockSpec((tm, KV_W), row),
                 pl.BlockSpec((tm, KV_W), row),
                 pl.BlockSpec((tm, hy_w), row),
                 pl.BlockSpec((tm, cf_w), row),
                 pl.BlockSpec((tm, g_w), row)]
    out_shape = [jax.ShapeDtypeStruct((t, ATTN_W), BF16),
                 jax.ShapeDtypeStruct((t, KV_W), BF16),
                 jax.ShapeDtypeStruct((t, KV_W), BF16),
                 jax.ShapeDtypeStruct((t, hy_w), F32),
                 jax.ShapeDtypeStruct((t, cf_w), F32),
                 jax.ShapeDtypeStruct((t, g_w), F32)]
    args = [x, mod, g1, w_in, cos, sin, qg, kg, hm]
    scratch = []
    if pending is not None:
        pos, mod_prev, ys = pending
        in_specs += [mod_spec, pl.BlockSpec(memory_space=pl.ANY)]
        out_specs.append(pl.BlockSpec((tm, d), row))
        out_shape.append(jax.ShapeDtypeStruct((t, d), F32))
        args = [pos] + args + [mod_prev, _grouped(ys)]
        scratch = [pltpu.VMEM((2, tm // SUBLANES, SUBLANES, d), F32), pltpu.SemaphoreType.DMA((2,))]
    return pl.pallas_call(
        functools.partial(_inproj_kernel, pending=pending is not None, n_steps=t // tm),
        grid_spec=pltpu.PrefetchScalarGridSpec(
            num_scalar_prefetch=0 if pending is None else 1,
            grid=(t // tm,),
            in_specs=in_specs, out_specs=out_specs, scratch_shapes=scratch),
        out_shape=out_shape,
        compiler_params=_cparams("arbitrary"),
        name="in_proj",
    )(*args)


def _attn_kernel(*refs, n_src):
    q_ref = refs[0]
    kv_refs = refs[1:1 + 2 * n_src]
    o_ref = refs[1 + 2 * n_src]
    kk_ref, vv_ref = refs[2 + 2 * n_src:]
    h = pl.program_id(1)
    lane = lax.broadcasted_iota(jnp.int32, (1, LANES), 1)
    lo_half = lane < HEAD_DIM

    @pl.when(pl.program_id(2) == 0)
    def _():
        keep = (lane // HEAD_DIM) == h
        off = 0
        for s in range(n_src):
            n = kv_refs[2 * s].shape[0]
            k = kv_refs[2 * s][...].astype(F32)
            kk_ref[off:off + n, :] = jnp.where(keep, k, pltpu.roll(k, HEAD_DIM, 1)).astype(BF16)
            v = kv_refs[2 * s + 1][...].astype(F32)
            v = jnp.where(keep, v, pltpu.roll(v, HEAD_DIM, 1))
            vv_ref[off:off + n, :] = jnp.where(lo_half, v, 1.0).astype(BF16)
            off += n

    kk = kk_ref[...]
    vv = vv_ref[...]
    n_heads = 2 * (q_ref.shape[1] // LANES)

    def scores(hd):
        q2 = q_ref[:, (hd // 2) * LANES:(hd // 2 + 1) * LANES]
        msk = lo_half if hd % 2 == 0 else jnp.logical_not(lo_half)
        qm = jnp.where(msk, q2, jnp.zeros_like(q2))
        return lax.dot_general(qm, kk, (((1,), (1,)), ((), ())), preferred_element_type=F32)

    def probs(s):
        return jnp.exp2(s - jnp.max(s, axis=-1, keepdims=True)).astype(BF16)

    def values(p):
        o = _dot(p, vv)
        return o / o[:, HEAD_DIM:HEAD_DIM + 1]

    s_all = {0: scores(0)}
    outs = {}
    p_prev = None
    for hd in range(n_heads):
        if hd + 1 < n_heads:
            s_all[hd + 1] = scores(hd + 1)
        p_cur = probs(s_all.pop(hd))
        if p_prev is not None:
            outs[hd - 1] = values(p_prev)
        p_prev = p_cur
    outs[n_heads - 1] = values(p_prev)
    for j in range(n_heads // 2):
        pair = jnp.where(lo_half, outs[2 * j], pltpu.roll(outs[2 * j + 1], HEAD_DIM, 1))
        o_ref[:, j * LANES:(j + 1) * LANES] = pair.astype(BF16)


def _attn_call(q, kvs, *, batch, tq):
    t = q.shape[0]
    nq = t // batch // tq
    gw = ATTN_W // N_KV_HEADS
    in_specs = [pl.BlockSpec((tq, gw), lambda b, h, i: (b * nq + i, h))]
    args = [q]
    nk = 0
    for k, v in kvs:
        n = k.shape[0] // batch
        nk += n
        for a in (k, v):
            in_specs.append(pl.BlockSpec((n, KV_W), lambda b, h, i: (b, 0)))
            args.append(a)
    return pl.pallas_call(
        functools.partial(_attn_kernel, n_src=len(kvs)),
        grid=(batch, N_KV_HEADS, nq),
        in_specs=in_specs,
        out_specs=pl.BlockSpec((tq, gw), lambda b, h, i: (b * nq + i, h)),
        out_shape=jax.ShapeDtypeStruct((t, ATTN_W), BF16),
        scratch_shapes=[pltpu.VMEM((nk, LANES), BF16), pltpu.VMEM((nk, LANES), BF16)],
        compiler_params=_cparams("arbitrary", "arbitrary", "arbitrary"),
        name="attention",
    )(*args)


def _hy_time_kernel(feat_ref, w1_ref, b1_ref, w2_ref, b2_ref, w3_ref, fq_ref, dec_ref,
                    sp_ref, sm_ref, kn_ref):
    n = feat_ref.shape[0]
    fq = fq_ref[0]
    h = jnp.sin(fq * (_dot3(feat_ref[...], w1_ref[0]) + b1_ref[0]))
    h = jnp.sin(fq * (_dot3(h, w2_ref[0]) + b2_ref[0]))
    h = _dot3(h, w3_ref[0]) * dec_ref[...]
    h = h / jnp.sum(jnp.abs(h), axis=0, keepdims=True)
    c = h.shape[1] // 2
    row = lax.broadcasted_iota(jnp.int32, (n, 1), 0)
    hf = h[:, 0:c]
    hb = jnp.where(row == 0, 0.0, h[:, c:2 * c])
    sp = hf + hb
    sp_ref[0] = sp
    sm_ref[0] = hf - hb
    sgn = (1 - 2 * (row & 1)).astype(F32)
    kn_ref[0] = jnp.sum(sp * sgn, axis=0, keepdims=True) * (0.5 / n)


def _hy_spec_kernel(fc_ref, fs_ref, sp_ref, sm_ref, kr_ref, ki_ref, *, n):
    tk = fc_ref.shape[0]
    k = pl.program_id(1) * tk + lax.broadcasted_iota(jnp.int32, (tk, 1), 0)
    scl = jnp.where(k == 0, 0.5 / n, 1.0 / n)
    sph, spl = _split(sp_ref[0])
    smh, sml = _split(sm_ref[0])
    kr = _dot(fc_ref[...], sph) + _dot(fc_ref[...], spl)
    ki = _dot(fs_ref[...], smh) + _dot(fs_ref[...], sml)
    kr_ref[0] = kr * scl
    ki_ref[0] = ki * scl


def _dft_tables(n):
    k = lax.broadcasted_iota(jnp.int32, (n, n), 0)
    t = lax.broadcasted_iota(jnp.int32, (n, n), 1)
    ang = ((k * t) % (2 * n)).astype(F32) * (math.pi / n)
    return jnp.cos(ang).astype(BF16), (-jnp.sin(ang)).astype(BF16)


def _hyena_features(n, c):
    t = jnp.linspace(0.0, 1.0, n, dtype=F32)[:, None]
    w = 2.0 * math.pi * jnp.arange(n, dtype=F32)[:, None] / n
    f = jnp.linspace(1e-4, HYENA_BANDS - 1, HYENA_BANDS, dtype=F32)[None, :]
    feat = jnp.concatenate([t, jnp.cos(f * w), -jnp.sin(f * w)], axis=-1)
    feat = jnp.pad(feat, ((0, 0), (0, LANES - feat.shape[1])))
    max_decay = math.log(HYENA_TARGET) / HYENA_FAST_DECAY
    min_decay = math.log(HYENA_TARGET) / HYENA_SLOW_DECAY
    deltas = jnp.tile(jnp.linspace(min_decay, max_decay, c, dtype=F32), 2)
    return feat, jnp.exp(-t * jnp.abs(deltas))


def _hyena_filter_call(n, tabs, w1p, b1, w2, b2, w3, fq):
    depth = w1p.shape[0]
    ffn = w2.shape[1]
    c = w3.shape[2] // 2
    feat, dec = _hyena_features(n, c)
    lay = lambda l: (l, 0, 0)
    const = lambda l: (0, 0)
    sp, sm, kn = pl.pallas_call(
        _hy_time_kernel,
        grid=(depth,),
        in_specs=[pl.BlockSpec((n, LANES), const),
                  pl.BlockSpec((1, LANES, ffn), lay), pl.BlockSpec((1, 1, ffn), lay),
                  pl.BlockSpec((1, ffn, ffn), lay), pl.BlockSpec((1, 1, ffn), lay),
                  pl.BlockSpec((1, ffn, 2 * c), lay), pl.BlockSpec((1, 1, ffn), lay),
                  pl.BlockSpec((n, 2 * c), const)],
        out_specs=[pl.BlockSpec((1, n, c), lay), pl.BlockSpec((1, n, c), lay),
                   pl.BlockSpec((1, 1, c), lay)],
        out_shape=[jax.ShapeDtypeStruct((depth, n, c), F32), jax.ShapeDtypeStruct((depth, n, c), F32),
                   jax.ShapeDtypeStruct((depth, 1, c), F32)],
        compiler_params=_cparams("arbitrary"),
        name="hyena_filter_time",
    )(feat, w1p, b1, w2, b2, w3, fq, dec)
    tk = min(n, 512)
    mat = pl.BlockSpec((tk, n), lambda l, j: (j, 0))
    vec = pl.BlockSpec((1, n, c), lambda l, j: (l, 0, 0))
    out = pl.BlockSpec((1, tk, c), lambda l, j: (l, j, 0))
    kr, ki = pl.pallas_call(
        functools.partial(_hy_spec_kernel, n=n),
        grid=(depth, n // tk),
        in_specs=[mat, mat, vec, vec],
        out_specs=[out, out],
        out_shape=[jax.ShapeDtypeStruct((depth, n, c), F32)] * 2,
        compiler_params=_cparams("arbitrary", "arbitrary"),
        name="hyena_filter_spectrum",
    )(*tabs, sp, sm)
    return kr, ki, kn


HYENA_ROW_CHUNK = 512


def _hyena_kernel(z_ref, sw_ref, sb_ref, fc_ref, fs_ref, kr_ref, ki_ref, kn_ref, db_ref, o_ref):
    n = z_ref.shape[0]
    c = o_ref.shape[1]
    z = z_ref[...]
    row = lax.broadcasted_iota(jnp.int32, (n, 1), 0)
    z_prev = jnp.where(row == 0, 0.0, pltpu.roll(z, 1, 0))
    z_next = jnp.where(row == n - 1, 0.0, pltpu.roll(z, n - 1, 0))
    zc = z_prev * sw_ref[0:1, :] + z * sw_ref[1:2, :] + z_next * sw_ref[2:3, :] + sb_ref[...]
    x0 = zc[:, 0:c]
    u = zc[:, 2 * c:3 * c] * zc[:, c:2 * c]
    ub = u.astype(BF16)
    sgn = (1 - 2 * (row & 1)).astype(F32)
    nyq = jnp.sum(u * sgn, axis=0, keepdims=True) * kn_ref[...]
    chunk = min(n, HYENA_ROW_CHUNK)
    chunks = [slice(r, r + chunk) for r in range(0, n, chunk)]
    pr, pi = [], []
    for rs in chunks:
        ur = _dot(fc_ref[rs, :], ub)
        ui = _dot(fs_ref[rs, :], ub)
        kr = kr_ref[rs, :]
        ki = ki_ref[rs, :]
        pr.append((ur * kr - ui * ki).astype(BF16))
        pi.append((ur * ki + ui * kr).astype(BF16))
    pr = jnp.concatenate(pr, axis=0)
    pi = jnp.concatenate(pi, axis=0)
    for rs in chunks:
        y = _dot(fc_ref[rs, :], pr) + _dot(fs_ref[rs, :], pi) + sgn[rs, :] * nyq
        o_ref[rs, :] = ((y + u[rs, :] * db_ref[...]) * x0[rs, :]).astype(BF16)


def _hyena_call(z, sw, sb, fc, fs, kr, ki, kn, db, *, n):
    t, w = z.shape
    c = w // 3
    const = lambda b: (0, 0)
    return pl.pallas_call(
        _hyena_kernel,
        grid=(t // n,),
        in_specs=[pl.BlockSpec((n, w), lambda b: (b, 0)),
                  pl.BlockSpec((3, w), const), pl.BlockSpec((1, w), const),
                  pl.BlockSpec((n, n), const), pl.BlockSpec((n, n), const),
                  pl.BlockSpec((n, c), const), pl.BlockSpec((n, c), const),
                  pl.BlockSpec((1, c), const), pl.BlockSpec((1, c), const)],
        out_specs=pl.BlockSpec((n, c), lambda b: (b, 0)),
        out_shape=jax.ShapeDtypeStruct((t, c), BF16),
        compiler_params=_cparams("arbitrary"),
        name="hyena_conv",
    )(z, sw, sb, fc, fs, kr, ki, kn, db)


CONF_PAD = 16


def _conf_kernel(z_ref, w_ref, b_ref, g_ref, beta_ref, o_ref, pad_ref):
    n = z_ref.shape[0]
    c = o_ref.shape[1]
    u = z_ref[:, 0:c] * jax.nn.sigmoid(z_ref[:, c:2 * c])
    zeros = jnp.zeros((CONF_PAD, c), F32)
    pad_ref[0:CONF_PAD, :] = zeros
    pad_ref[CONF_PAD + n:2 * CONF_PAD + n, :] = zeros
    pad_ref[CONF_PAD:CONF_PAD + n, :] = u
    first = CONF_PAD - (CONF_K - 1) // 2
    acc = b_ref[...]
    for r in range(SUBLANES):
        part = None
        for off in range(r, first + CONF_K, SUBLANES):
            j = off - first
            if j < 0:
                continue
            term = w_ref[j:j + 1, :] * pad_ref[off - r:off - r + n + SUBLANES, :]
            part = term if part is None else part + term
        acc = acc + part[r:r + n, :]
    mu = jnp.mean(acc, axis=-1, keepdims=True)
    dlt = acc - mu
    var = jnp.mean(dlt * dlt, axis=-1, keepdims=True)
    y = dlt * lax.rsqrt(var + NORM_EPS) * g_ref[...] + beta_ref[...]
    o_ref[...] = _silu(y).astype(BF16)


def _conf_call(z, w, b, g, beta, *, n):
    t, w2 = z.shape
    c = w2 // 2
    const = lambda i: (0, 0)
    return pl.pallas_call(
        _conf_kernel,
        grid=(t // n,),
        in_specs=[pl.BlockSpec((n, w2), lambda i: (i, 0)),
                  pl.BlockSpec((CONF_K, c), const), pl.BlockSpec((1, c), const),
                  pl.BlockSpec((1, c), const), pl.BlockSpec((1, c), const)],
        out_specs=pl.BlockSpec((n, c), lambda i: (i, 0)),
        out_shape=jax.ShapeDtypeStruct((t, c), BF16),
        scratch_shapes=[pltpu.VMEM((n + 2 * CONF_PAD, c), F32)],
        compiler_params=_cparams("arbitrary"),
        name="conformer_conv",
    )(z, w, b, g, beta)


def _route(sel, s):
    grp_score = []
    for g in range(N_GROUPS):
        r = sel[g * GROUP_SIZE:(g + 1) * GROUP_SIZE]
        best = None
        for i in range(GROUP_SIZE):
            for j in range(i + 1, GROUP_SIZE):
                pair = r[i] + r[j]
                best = pair if best is None else jnp.maximum(best, pair)
        grp_score.append(best)
    gbest = grp_score[0]
    gidx = jnp.zeros_like(gbest, dtype=jnp.int32)
    for g in range(1, N_GROUPS):
        better = grp_score[g] > gbest
        gidx = jnp.where(better, g, gidx)
        gbest = jnp.where(better, grp_score[g], gbest)
    neg = jnp.full_like(gbest, -jnp.inf)
    masked = [jnp.where(gidx == e // GROUP_SIZE, sel[e], neg) for e in range(N_EXPERTS)]

    def argmax_first(vals):
        best = vals[0]
        idx = jnp.zeros_like(gidx)
        for e in range(1, N_EXPERTS):
            better = vals[e] > best
            idx = jnp.where(better, e, idx)
            best = jnp.where(better, vals[e], best)
        return idx

    i1 = argmax_first(masked)
    i2 = argmax_first([jnp.where(i1 == e, neg, masked[e]) for e in range(N_EXPERTS)])
    zero = jnp.zeros_like(gbest)
    w1 = sum(jnp.where(i1 == e, s[e], zero) for e in range(N_EXPERTS))
    w2 = sum(jnp.where(i2 == e, s[e], zero) for e in range(N_EXPERTS))
    tot = w1 + w2
    w1, w2 = w1 / tot, w2 / tot
    swap = i2 < i1
    a = jnp.where(swap, i2, i1) - GROUP_SIZE * gidx
    b = jnp.where(swap, i1, i2) - GROUP_SIZE * gidx
    cls = PAIRS_PER_GROUP * gidx + ((a * (2 * GROUP_SIZE - 1 - a)) >> 1) + (b - a - 1)
    return cls, jnp.where(swap, w2, w1), jnp.where(swap, w1, w2)


PAIRS_PER_GROUP = GROUP_SIZE * (GROUP_SIZE - 1) // 2
N_CLASSES = N_GROUPS * PAIRS_PER_GROUP
SORT_TILE = 512


def _class_experts():
    lo, hi = [], []
    for g in range(N_GROUPS):
        for a in range(GROUP_SIZE):
            for b in range(a + 1, GROUP_SIZE):
                lo.append(g * GROUP_SIZE + a)
                hi.append(g * GROUP_SIZE + b)
    return jnp.asarray(lo + hi, dtype=jnp.int32)


def _col(row):
    return jnp.broadcast_to(row, (LANES, row.shape[1])).T[:, 0:1]


def _row(col):
    return jnp.broadcast_to(col, (col.shape[0], LANES)).T[0:1, :]


STATE_ROWS = 8
STATE_LANES = 256


def _initial_state():
    lane = lax.broadcasted_iota(jnp.int32, (1, STATE_LANES), 1)
    zero = jnp.zeros((1, STATE_LANES), jnp.int32)
    rows = [zero, jnp.where(lane < N_CLASSES, lane, 0), zero + N_CLASSES,
            jnp.where(lane < N_CLASSES, lane, LANES)] + [zero] * (STATE_ROWS - 4)
    return jnp.concatenate(rows, axis=0).astype(F32)


def _allocate_rows(cls, st_ref):
    tm = cls.shape[1]
    cls_col = _col(cls.astype(F32)).astype(jnp.int32)
    lane = lax.broadcasted_iota(jnp.int32, (1, LANES), 1)
    onehot = (cls_col == lane).astype(F32)
    counts = jnp.sum(onehot, axis=0, keepdims=True)
    t = lax.broadcasted_iota(jnp.int32, (tm, tm), 0)
    s = lax.broadcasted_iota(jnp.int32, (tm, tm), 1)
    earlier = _dot((s < t).astype(BF16), onehot.astype(BF16))
    fill = st_ref[0:1, 0:LANES]
    cur = st_ref[1:2, 0:LANES]
    nfree = st_ref[2:3, 0:LANES]
    tot = fill + counts
    full = tot >= SORT_TILE
    full_f = full.astype(F32)
    r = lax.broadcasted_iota(jnp.int32, (LANES, LANES), 0)
    c = lax.broadcasted_iota(jnp.int32, (LANES, LANES), 1)
    before = _dot(jnp.broadcast_to(full_f, (SUBLANES, LANES)).astype(BF16), (r < c).astype(BF16))[0:1, :]
    newid = nfree + before
    rank = jnp.sum(onehot * (fill + earlier), axis=1, keepdims=True)
    cur_base = jnp.sum(onehot * cur, axis=1, keepdims=True) * SORT_TILE
    new_base = jnp.sum(onehot * newid, axis=1, keepdims=True) * SORT_TILE
    pos_col = jnp.where(rank < SORT_TILE, cur_base + rank, new_base + (rank - SORT_TILE))
    st_ref[0:1, 0:LANES] = jnp.where(full, tot - SORT_TILE, tot)
    st_ref[1:2, 0:LANES] = jnp.where(full, newid, cur)
    st_ref[2:3, :] = st_ref[2:3, :] + jnp.sum(full_f, axis=1, keepdims=True)
    chunk = lax.broadcasted_iota(jnp.int32, (1, STATE_LANES), 1).astype(F32)
    opened = jnp.logical_and(_col(newid) == chunk, _col(full_f) > 0.0)
    cls_sub = lax.broadcasted_iota(jnp.int32, (LANES, 1), 0).astype(F32)
    chunk_cls = jnp.sum(jnp.where(opened, cls_sub, 0.0), axis=0, keepdims=True)
    hit = jnp.sum(opened.astype(F32), axis=0, keepdims=True)
    st_ref[3:4, :] = jnp.where(hit > 0.0, chunk_cls, st_ref[3:4, :])
    return _row(pos_col).astype(jnp.int32)


def _merge_kernel(oa_ref, oh_ref, oc_ref, zg_ref, x_ref, mod_ref, g2_ref,
                  pa_ref, ph_ref, pc_ref, wo_ref, wr_ref, br_ref, state_ref, buf_in,
                  xo_ref, pos_ref, state_out_ref, buf_out,
                  hxs, posv, poss, st, row_sem, pos_sem, *, n_steps):
    del buf_in
    tm, d = x_ref.shape
    groups = tm // SUBLANES
    i = pl.program_id(0)
    last = n_steps - 1
    slot = i % 2
    other = (i + 1) % 2
    dump_row = buf_out.shape[0] * SUBLANES - tm

    def row_copy(sl):
        def build(g, j):
            return pltpu.make_async_copy(hxs.at[sl, g, pl.ds(j, 1)],
                                         _row_at(buf_out, poss[sl, g * SUBLANES + j]), row_sem.at[sl])
        return build

    def pos_copy(sl):
        return pltpu.make_async_copy(posv.at[0], poss.at[sl], pos_sem)

    @pl.when(i == 0)
    def _():
        st[...] = state_ref[...]
        hxs[1] = jnp.zeros(hxs.shape[1:], F32)

        def fill_dump(r, carry):
            poss[1, r] = dump_row + r
            return carry

        lax.fori_loop(0, tm, fill_dump, 0)

    @pl.when(i >= 1)
    def _():
        pos_copy(other).wait()
        _row_copies(groups, row_copy(slot), "wait")

    send_prev = row_copy(other)

    def issue_prev(part, parts):
        for g in range(groups * part // parts, groups * (part + 1) // parts):
            for j in range(SUBLANES):
                send_prev(g, j).start(priority=j % 2)

    merged = jax.nn.sigmoid(zg_ref[:, 0:d]) * _dot(oa_ref[...], pa_ref[...])
    issue_prev(0, 4)
    merged += jax.nn.sigmoid(zg_ref[:, d:2 * d]) * _dot(oh_ref[...], ph_ref[...])
    issue_prev(1, 4)
    merged += jax.nn.sigmoid(zg_ref[:, 2 * d:3 * d]) * _dot(oc_ref[...], pc_ref[...])
    issue_prev(2, 4)
    y = _dot(merged.astype(BF16), wo_ref[...])
    issue_prev(3, 4)
    x = x_ref[...] + mod_ref[0, :, 2 * d:3 * d] * y
    xo_ref[...] = x
    ms = jnp.mean(x * x, axis=-1, keepdims=True)
    h2 = x * lax.rsqrt(ms + NORM_EPS) * g2_ref[...]
    h2 = h2 * (1.0 + mod_ref[0, :, 4 * d:5 * d]) + mod_ref[0, :, 3 * d:4 * d]
    hxs[slot, :, :, 0:d] = h2.reshape(groups, SUBLANES, d)
    hh, hl = _split(h2)
    wh, wl = _split(wr_ref[...])
    nt = (((1,), (1,)), ((), ()))
    logits = (lax.dot_general(wh, hh, nt, preferred_element_type=F32)
              + (lax.dot_general(wh, hl, nt, preferred_element_type=F32)
                 + lax.dot_general(wl, hh, nt, preferred_element_type=F32)))
    s = jax.nn.sigmoid(logits)
    sel = s + br_ref[...]
    cls, w_lo, w_hi = _route([sel[e:e + 1, :] for e in range(N_EXPERTS)],
                             [s[e:e + 1, :] for e in range(N_EXPERTS)])
    sub = lax.broadcasted_iota(jnp.int32, (LANES, tm), 0)
    rows = jnp.where(sub == 0, w_lo, jnp.where(sub == 1, w_hi, 0.0))
    hxs[slot, :, :, d:d + LANES] = rows.T.reshape(groups, SUBLANES, LANES)
    pos = _allocate_rows(cls, st)
    pos_ref[0] = pos
    posv[...] = pos
    pos_copy(slot).start()

    @pl.when(i == last)
    def _():
        pos_copy(slot).wait()
        _row_copies(groups, row_copy(slot), "start")
        _row_copies(groups, send_prev, "wait")
        _row_copies(groups, row_copy(slot), "wait")
        state_out_ref[...] = st[...]


def _merge_call(oa, oh, oc, zg, x, mod, g2, pa, ph, pc, wo, wr_t, br, state, buf, *, tm):
    t, d = x.shape
    w = d + LANES
    tiles_per_mod = t // mod.shape[0] // tm
    groups = tm // SUBLANES
    row = lambda i: (i, 0)
    const = lambda i: (0, 0)
    full = lambda a: pl.BlockSpec(a.shape, const)
    any_spec = pl.BlockSpec(memory_space=pl.ANY)
    x1, pos, state, buf = pl.pallas_call(
        functools.partial(_merge_kernel, n_steps=t // tm),
        grid=(t // tm,),
        in_specs=[pl.BlockSpec((tm, oa.shape[1]), row), pl.BlockSpec((tm, oh.shape[1]), row),
                  pl.BlockSpec((tm, oc.shape[1]), row), pl.BlockSpec((tm, zg.shape[1]), row),
                  pl.BlockSpec((tm, d), row),
                  pl.BlockSpec((1, 1, mod.shape[2]), lambda i: (i // tiles_per_mod, 0, 0)),
                  full(g2), full(pa), full(ph), full(pc), full(wo), full(wr_t), full(br),
                  full(state), any_spec],
        out_specs=[pl.BlockSpec((tm, d), row), pl.BlockSpec((1, 1, tm), lambda i: (i, 0, 0)),
                   full(state), any_spec],
        out_shape=[jax.ShapeDtypeStruct((t, d), F32), jax.ShapeDtypeStruct((t // tm, 1, tm), jnp.int32),
                   jax.ShapeDtypeStruct(state.shape, F32),
                   jax.ShapeDtypeStruct((buf.shape[0] // SUBLANES, SUBLANES, w), F32)],
        input_output_aliases={14: 3},
        scratch_shapes=[pltpu.VMEM((2, groups, SUBLANES, w), F32), pltpu.VMEM((1, tm), jnp.int32),
                        pltpu.SMEM((2, tm), jnp.int32), pltpu.VMEM(state.shape, F32),
                        pltpu.SemaphoreType.DMA((2,)), pltpu.SemaphoreType.DMA],
        compiler_params=_cparams("arbitrary"),
        name="merge_router",
    )(oa, oh, oc, zg, x, mod, g2, pa, ph, pc, wo, wr_t, br, state, _grouped(buf))
    return x1, pos.reshape(t), state, buf.reshape(-1, w)


def _ffn_kernel(tcls_ref, etab_ref, xs_ref, wga_ref, wua_ref, wda_ref, wgb_ref, wub_ref, wdb_ref, o_ref):
    del etab_ref
    d = o_ref.shape[1]
    valid = tcls_ref[pl.program_id(0)] < N_CLASSES

    @pl.when(valid)
    def _():
        x = xs_ref[:, 0:d].astype(BF16)
        ha = _silu(_dot(x, wga_ref[0])) * _dot(x, wua_ref[0]) * xs_ref[:, d:d + 1]
        hb = _silu(_dot(x, wgb_ref[0])) * _dot(x, wub_ref[0]) * xs_ref[:, d + 1:d + 2]
        o_ref[...] = _dot(ha.astype(BF16), wda_ref[0]) + _dot(hb.astype(BF16), wdb_ref[0])

    @pl.when(jnp.logical_not(valid))
    def _():
        o_ref[...] = jnp.zeros_like(o_ref)


def _ffn_call(tcls, etab, buf, wg, wu, wd):
    rows, w = tcls.shape[0] * SORT_TILE, buf.shape[1]
    ne, d, ff = wg.shape

    def expert(which):
        def index(j, tcls_ref, etab_ref):
            c = jnp.minimum(tcls_ref[j], N_CLASSES - 1)
            return (etab_ref[which * N_CLASSES + c], 0, 0)
        return index

    up = lambda which: pl.BlockSpec((1, d, ff), expert(which))
    down = lambda which: pl.BlockSpec((1, ff, d), expert(which))
    return pl.pallas_call(
        _ffn_kernel,
        grid_spec=pltpu.PrefetchScalarGridSpec(
            num_scalar_prefetch=2,
            grid=(rows // SORT_TILE,),
            in_specs=[pl.BlockSpec((SORT_TILE, w), lambda j, *_: (j, 0)),
                      up(0), up(0), down(0), up(1), up(1), down(1)],
            out_specs=pl.BlockSpec((SORT_TILE, d), lambda j, *_: (j, 0))),
        out_shape=jax.ShapeDtypeStruct((rows, d), F32),
        compiler_params=_cparams("arbitrary"),
        name="moe_ffn",
    )(tcls, etab, buf, wg, wu, wd, wg, wu, wd)


def _combine_kernel(pos_ref, x_ref, mod_ref, ys_hbm, o_ref, ybuf, sem, *, n_steps):
    tm, d = x_ref.shape
    groups = tm // SUBLANES
    i = pl.program_id(0)

    def gather(tile, op):
        slot = tile % 2

        def copy_of_row(g, j):
            p = pos_ref[tile * tm + g * SUBLANES + j]
            return pltpu.make_async_copy(_row_at(ys_hbm, p), ybuf.at[slot, g, pl.ds(j, 1)],
                                         sem.at[slot])

        _row_copies(groups, copy_of_row, op)

    @pl.when(i == 0)
    def _():
        gather(i, "start")

    @pl.when(i + 1 < n_steps)
    def _():
        gather(i + 1, "start")

    gather(i, "wait")
    o_ref[...] = x_ref[...] + mod_ref[0, :, 5 * d:6 * d] * ybuf[i % 2].reshape(tm, d)


def _combine_call(pos, x, mod, ys, *, tm):
    t, d = x.shape
    tiles_per_mod = t // mod.shape[0] // tm
    return pl.pallas_call(
        functools.partial(_combine_kernel, n_steps=t // tm),
        grid_spec=pltpu.PrefetchScalarGridSpec(
            num_scalar_prefetch=1,
            grid=(t // tm,),
            in_specs=[pl.BlockSpec((tm, d), lambda i, *_: (i, 0)),
                      pl.BlockSpec((1, 1, mod.shape[2]), lambda i, *_: (i // tiles_per_mod, 0, 0)),
                      pl.BlockSpec(memory_space=pl.ANY)],
            out_specs=pl.BlockSpec((tm, d), lambda i, *_: (i, 0)),
            scratch_shapes=[pltpu.VMEM((2, tm // SUBLANES, SUBLANES, d), F32),
                            pltpu.SemaphoreType.DMA((2,))]),
        out_shape=jax.ShapeDtypeStruct((t, d), F32),
        compiler_params=_cparams("arbitrary"),
        name="moe_combine",
    )(pos, x, mod, _grouped(ys))


def _rope_tables(n_tokens):
    rows = n_tokens // GRID_W
    row_ids = jnp.repeat(jnp.arange(rows), GRID_W).astype(F32)
    col_ids = jnp.tile(jnp.arange(GRID_W), rows).astype(F32)
    half = HEAD_DIM // 2
    inv = ROPE_THETA ** (-jnp.arange(0, half, 2, dtype=F32) / half)
    ang_r = row_ids[:, None] * inv
    ang_c = col_ids[:, None] * inv
    ang = jnp.concatenate([ang_r, ang_r, ang_c, ang_c] * (LANES // HEAD_DIM), axis=-1)
    return jnp.cos(ang), jnp.sin(ang)


def _head_mean_matrix():
    i = lax.broadcasted_iota(jnp.int32, (LANES, LANES), 0) // HEAD_DIM
    j = lax.broadcasted_iota(jnp.int32, (LANES, LANES), 1) // HEAD_DIM
    return jnp.where(i == j, 1.0 / HEAD_DIM, 0.0).astype(BF16)


def kernel(x, c, ctx, c_ctx, w_mod, b_mod, g_norm1, g_norm2, w_in, q_norm, k_norm, hy_short_w, hy_short_b, hy_w1, hy_b1, hy_w2, hy_b2, hy_w3, hy_freq, hy_bias, cf_dw_w, cf_dw_b, cf_ln_g, cf_ln_b, p_attn, p_hyena, p_conv, w_out, w_router, b_router, w_exp_gate, w_exp_up, w_exp_down):
    batch, seq, d = x.shape
    n_ctx = ctx.shape[1]
    depth = w_mod.shape[0]
    hy_w = hy_short_w.shape[2]
    cf_w = 2 * cf_dw_w.shape[2]
    tm = 512

    pad_rows = (-(batch + 1)) % 8
    cc = jnp.concatenate([c, c_ctx[None, :], jnp.zeros((pad_rows, d), F32)], axis=0)
    mods = _mod_call(cc, w_mod, b_mod)

    cos_x, sin_x = _rope_tables(seq)
    cos_c = jnp.ones((tm, LANES), F32)
    sin_c = jnp.zeros((tm, LANES), F32)
    hm = _head_mean_matrix()
    tile2 = lambda g: jnp.tile(g, (1, LANES // HEAD_DIM))
    qg, kg = tile2(q_norm), tile2(k_norm)

    w1p = jnp.pad(hy_w1, ((0, 0), (0, LANES - hy_w1.shape[1]), (0, 0)))
    r3 = lambda a: a[:, None, :]
    tabs_x = _dft_tables(seq)
    tabs_c = _dft_tables(n_ctx)
    filt_args = (w1p, r3(hy_b1), hy_w2, r3(hy_b2), hy_w3, r3(hy_freq))
    kr_x, ki_x, kn_x = _hyena_filter_call(seq, tabs_x, *filt_args)
    kr_c, ki_c, kn_c = _hyena_filter_call(n_ctx, tabs_c, *filt_args)

    bf = lambda a: a.astype(BF16)
    w_in_b, pa_b, ph_b, pc_b, wo_b = bf(w_in), bf(p_attn), bf(p_hyena), bf(p_conv), bf(w_out)
    wg_b, wu_b, wd_b = bf(w_exp_gate), bf(w_exp_up), bf(w_exp_down)
    wr_t = w_router.T
    br = b_router[:, None]

    xs = [x.reshape(batch * seq, d), ctx.reshape(batch * n_ctx, d)]
    pending = [None, None]
    for l in range(depth):
        last = l == depth - 1
        mods_l = [mods[l, :batch][:, None, :], mods[l, batch:batch + 1][:, None, :]]
        row = lambda a: a[l][None, :]
        consts = [(seq, cos_x, sin_x, tabs_x, kr_x, ki_x, kn_x),
                  (n_ctx, cos_c, sin_c, tabs_c, kr_c, ki_c, kn_c)]
        proj = []
        for si in range(2):
            out = _inproj_call(xs[si], mods_l[si], row(g_norm1), w_in_b[l], consts[si][1], consts[si][2],
                               qg[l:l + 1], kg[l:l + 1], hm, pending[si], tm=tm, hy_w=hy_w, cf_w=cf_w)
            if pending[si] is not None:
                xs[si] = out[-1]
            proj.append(out[:6])
        (q_x, k_x, v_x, zhy_x, zcf_x, zg_x), (q_c, k_c, v_c, zhy_c, zcf_c, zg_c) = proj
        n_streams = 1 if last else 2
        n_sorted_tiles = sum(a.shape[0] for a in xs[:n_streams]) // SORT_TILE + N_CLASSES
        assert n_sorted_tiles <= STATE_LANES
        state = _initial_state()
        buf = jnp.zeros(((n_sorted_tiles + 1) * SORT_TILE, d + LANES), F32)
        x1s, poss = [], []
        for si in range(n_streams):
            n, _, _, tabs, kr, ki, kn = consts[si]
            if si == 0:
                oa = _attn_call(q_x, [(k_x, v_x), (k_c, v_c)], batch=batch, tq=512)
                zhy, zcf, zg = zhy_x, zcf_x, zg_x
            else:
                oa = _attn_call(q_c, [(k_c, v_c)], batch=batch, tq=n_ctx)
                zhy, zcf, zg = zhy_c, zcf_c, zg_c
            oh = _hyena_call(zhy, hy_short_w[l], row(hy_short_b), tabs[0], tabs[1], kr[l], ki[l], kn[l],
                             row(hy_bias), n=n)
            oc = _conf_call(zcf, cf_dw_w[l], row(cf_dw_b), row(cf_ln_g), row(cf_ln_b), n=n)
            x1, pos, state, buf = _merge_call(oa, oh, oc, zg, xs[si], mods_l[si], row(g_norm2), pa_b[l],
                                              ph_b[l], pc_b[l], wo_b[l], wr_t, br, state, buf, tm=tm)
            x1s.append(x1)
            poss.append(pos)
        tcls = state[3, :n_sorted_tiles].astype(jnp.int32)
        ys = _ffn_call(tcls, _class_experts(), buf, wg_b[l], wu_b[l], wd_b[l])
        if last:
            return _combine_call(poss[0], x1s[0], mods_l[0], ys, tm=tm).reshape(batch, seq, d)
        for si in range(2):
            xs[si] = x1s[si]
            pending[si] = (poss[si], mods_l[si], ys)
```
